```python
import jax, jax.numpy as jnp
from jax import lax
import numpy as np

D_MODEL = 1024
BATCH = 8
SEQ = 4096
DEPTH = 1

HEAD_DIM = 64
DIL_PAIRS = ((128, 1), (512, 4), (2048, 16))
N_DIL_GROUPS = len(DIL_PAIRS)
HEADS_PER_DIL_GROUP = 4
N_DIL_HEADS = N_DIL_GROUPS * HEADS_PER_DIL_GROUP
DIL_WIDTH = N_DIL_HEADS * HEAD_DIM
DIL_OUT_WIDTH = HEADS_PER_DIL_GROUP * HEAD_DIM
N_FOX_HEADS = 8
FOX_WIDTH = N_FOX_HEADS * HEAD_DIM
FOX_BLOCK = 128
N_BRANCHES = 2
D_FF = -(-8 * D_MODEL // (3 * 256)) * 256
RMS_EPS = 1e-6
NEG_INF = -1e30
ATTN_SCALE = HEAD_DIM ** -0.5
IN_COLS = 3 * DIL_WIDTH + 3 * FOX_WIDTH + N_FOX_HEADS + N_BRANCHES * D_MODEL

kernel_name = 'hybrid_dilated_fox_gated_block'


def rms_norm(x, g):
    xf = x.astype(jnp.float32)
    y = xf * lax.rsqrt(jnp.mean(xf * xf, axis=-1, keepdims=True) + RMS_EPS)
    return (y * g.astype(jnp.float32)).astype(x.dtype)


def alibi_slopes(n):
    return 2.0 ** (-8.0 * jnp.arange(1, n + 1, dtype=jnp.float32) / n)


def dilated_window_attention(q, k, v, window, dilation, slopes):
    b, s, h, dh = q.shape
    w = window // dilation
    span = w * dilation
    s_pad = -(-s // span) * span
    nb = s_pad // span
    pad = ((0, 0), (0, s_pad - s), (0, 0), (0, 0))

    def to_blocks(t):
        return jnp.pad(t, pad).reshape(b, nb, w, dilation, h, dh)

    def with_prev(t):
        prev = jnp.pad(t, ((0, 0), (1, 0), (0, 0), (0, 0), (0, 0), (0, 0)))[:, :-1]
        return jnp.concatenate([prev, t], axis=2)

    qb = to_blocks(q)
    kw = with_prev(to_blocks(k))
    vw = with_prev(to_blocks(v))
    scores = jnp.einsum('bnqrhd,bnkrhd->bnrhqk', qb, kw,
                        preferred_element_type=jnp.float32) * ATTN_SCALE
    qi = jnp.arange(w)[:, None]
    ki = jnp.arange(2 * w)[None, :] - w
    rel = qi - ki
    glob = jnp.arange(nb)[:, None] * w + ki
    valid = ((rel >= 0) & (rel <= w))[None] & (glob >= 0)[:, None, :]
    bias = -slopes[:, None, None] * (rel * dilation).astype(jnp.float32)[None]
    logits = jnp.where(valid[None, :, None, None], scores + bias[None, None, None], NEG_INF)
    m = jnp.max(logits, axis=-1, keepdims=True)
    p = jnp.exp(logits - m)
    denom = jnp.sum(p, axis=-1, keepdims=True)
    lse = (m + jnp.log(denom))[..., 0]
    o = jnp.einsum('bnrhqk,bnkrhd->bnqrhd', (p / denom).astype(v.dtype), vw,
                   preferred_element_type=jnp.float32)
    out = o.reshape(b, s_pad, h, dh)[:, :s]
    lse = lse.transpose(0, 1, 4, 2, 3).reshape(b, s_pad, h)[:, :s]
    return out, lse


def forgetting_attention(q, k, v, log_f):
    b, s, h, dh = q.shape
    c = jnp.cumsum(log_f, axis=1)
    s_pad = -(-s // FOX_BLOCK) * FOX_BLOCK
    nblk = s_pad // FOX_BLOCK
    qb = jnp.pad(q, ((0, 0), (0, s_pad - s), (0, 0), (0, 0)))
    qb = qb.reshape(b, nblk, FOX_BLOCK, h, dh).transpose(1, 0, 2, 3, 4)
    cqb = jnp.pad(c, ((0, 0), (0, s_pad - s), (0, 0)))
    cqb = cqb.reshape(b, nblk, FOX_BLOCK, h).transpose(1, 0, 3, 2)
    ck = c.transpose(0, 2, 1)
    starts = jnp.arange(nblk) * FOX_BLOCK
    kpos = jnp.arange(s)

    def block(args):
        q_blk, c_blk, start = args
        sc = jnp.einsum('bqhd,bkhd->bhqk', q_blk, k,
                        preferred_element_type=jnp.float32) * ATTN_SCALE
        decay = c_blk[..., None] - ck[:, :, None, :]
        qpos = start + jnp.arange(FOX_BLOCK)
        mask = kpos[None, :] <= qpos[:, None]
        p = jax.nn.softmax(jnp.where(mask, sc + decay, NEG_INF), axis=-1)
        return jnp.einsum('bhqk,bkhd->bqhd', p.astype(v.dtype), v,
                          preferred_element_type=jnp.float32)

    out = lax.map(block, (qb, cqb, starts))
    return out.transpose(1, 0, 2, 3, 4).reshape(b, s_pad, h, dh)[:, :s]


def setup_inputs(seed: int = 0) -> dict:
    key = jax.random.key(seed)
    ks = jax.random.split(key, 13)
    f32 = jnp.float32

    def dense(k, shape, fan_in, gain=1.0):
        return jax.random.normal(k, shape, f32) * (gain * fan_in ** -0.5)

    return {
        'x': jax.random.normal(ks[0], (BATCH, SEQ, D_MODEL), f32),
        'norm_mix_g': 1.0 + 0.01 * jax.random.normal(ks[1], (DEPTH, D_MODEL), f32),
        'w_in': dense(ks[2], (DEPTH, D_MODEL, IN_COLS), D_MODEL),
        'b_fgt': jax.random.uniform(ks[3], (DEPTH, N_FOX_HEADS), f32, 1.0, 4.0),
        'b_gate': 0.01 * jax.random.normal(ks[4], (DEPTH, N_BRANCHES * D_MODEL), f32),
        'w_dil_out': dense(ks[5], (DEPTH, DIL_OUT_WIDTH, D_MODEL), DIL_OUT_WIDTH),
        'w_fox_out': dense(ks[6], (DEPTH, FOX_WIDTH, D_MODEL), FOX_WIDTH),
        'w_out': dense(ks[7], (DEPTH, D_MODEL, D_MODEL), D_MODEL),
        'norm_ffn_g': 1.0 + 0.01 * jax.random.normal(ks[8], (DEPTH, D_MODEL), f32),
        'w_ffn_in': dense(ks[9], (DEPTH, D_MODEL, 2 * D_FF), D_MODEL),
        'w_ffn_down': dense(ks[10], (DEPTH, D_FF, D_MODEL), D_FF),
        'norm_final_g': 1.0 + 0.01 * jax.random.normal(ks[11], (D_MODEL,), f32),
    }


def reference(x, norm_mix_g, w_in, b_fgt, b_gate, w_dil_out, w_fox_out, w_out,
              norm_ffn_g, w_ffn_in, w_ffn_down, norm_final_g):
    b, s, _ = x.shape
    slopes = alibi_slopes(N_DIL_HEADS).reshape(N_DIL_GROUPS, HEADS_PER_DIL_GROUP)
    o1 = DIL_WIDTH
    o2 = 3 * DIL_WIDTH
    o3 = o2 + 3 * FOX_WIDTH
    o4 = o3 + N_FOX_HEADS
    for layer in range(DEPTH):
        h = rms_norm(x, norm_mix_g[layer])
        proj = h @ w_in[layer]
        dil = proj[..., :o2].reshape(b, s, 3, N_DIL_GROUPS, HEADS_PER_DIL_GROUP, HEAD_DIM)
        fox = proj[..., o2:o3].reshape(b, s, 3, N_FOX_HEADS, HEAD_DIM)
        f_logit = proj[..., o3:o4].astype(jnp.float32) + b_fgt[layer]
        gates = jax.nn.sigmoid(proj[..., o4:].astype(jnp.float32) + b_gate[layer])
        g_a, g_b = gates[..., :D_MODEL], gates[..., D_MODEL:]

        outs, lses = [], []
        for g, (window, dilation) in enumerate(DIL_PAIRS):
            o_g, lse_g = dilated_window_attention(dil[:, :, 0, g], dil[:, :, 1, g],
                                                  dil[:, :, 2, g], window, dilation, slopes[g])
            outs.append(o_g)
            lses.append(lse_g)
        alpha = jax.nn.softmax(jnp.stack(lses, axis=0), axis=0)
        o_a = jnp.sum(alpha[..., None] * jnp.stack(outs, axis=0), axis=0)
        y_a = o_a.reshape(b, s, DIL_OUT_WIDTH).astype(x.dtype) @ w_dil_out[layer]

        o_b = forgetting_attention(fox[:, :, 0], fox[:, :, 1], fox[:, :, 2],
                                   jax.nn.log_sigmoid(f_logit))
        y_b = o_b.reshape(b, s, FOX_WIDTH).astype(x.dtype) @ w_fox_out[layer]

        merged = (g_a * y_a + g_b * y_b).astype(x.dtype)
        x = x + merged @ w_out[layer]

        h2 = rms_norm(x, norm_ffn_g[layer])
        gu = h2 @ w_ffn_in[layer]
        x = x + (jax.nn.silu(gu[..., :D_FF]) * gu[..., D_FF:]) @ w_ffn_down[layer]
    return rms_norm(x, norm_final_g)
```

```python
import functools

import jax
import jax.numpy as jnp
from jax import lax
from jax.experimental import pallas as pl
from jax.experimental.pallas import tpu as pltpu

D_MODEL = 1024
HEAD_DIM = 64
DIL_PAIRS = ((128, 1), (512, 4), (2048, 16))
N_DIL_GROUPS = len(DIL_PAIRS)
HEADS_PER_DIL_GROUP = 4
N_DIL_HEADS = N_DIL_GROUPS * HEADS_PER_DIL_GROUP
DIL_OUT_WIDTH = HEADS_PER_DIL_GROUP * HEAD_DIM
DIL_WIDTH = N_DIL_HEADS * HEAD_DIM
N_FOX_HEADS = 8
FOX_WIDTH = N_FOX_HEADS * HEAD_DIM
D_FF = 2816
RMS_EPS = 1e-6
NEG_INF = -1e30
ATTN_SCALE = HEAD_DIM ** -0.5

LANES = 128
DIL_BLOCK = 128
VMEM_LIMIT = 56 * 1024 * 1024

F32 = jnp.float32
BF16 = jnp.bfloat16


def _dot(a, b):
    return jnp.dot(a, b, preferred_element_type=F32)


def _dot_nt(a, b):
    return lax.dot_general(a, b, (((1,), (1,)), ((), ())), preferred_element_type=F32)


def _rms(x, g):
    ms = jnp.mean(x * x, axis=-1, keepdims=True)
    return x * lax.rsqrt(ms + RMS_EPS) * g


def _inproj_kernel(x_ref, g_ref, wa0_ref, wa1_ref, wa2_ref, wfq_ref, wfk_ref, wfv_ref,
                   wf_ref, bf_ref, wg_ref, bg_ref,
                   a0_ref, a1_ref, a2_ref, fq_ref, fk_ref, fv_ref, c_ref, gate_ref,
                   carry_ref):
    tm = x_ref.shape[1]

    @pl.when(pl.program_id(1) == 0)
    def _():
        carry_ref[...] = jnp.zeros_like(carry_ref)

    h = _rms(x_ref[0], g_ref[...]).astype(BF16)
    for w_ref, o_ref in ((wa0_ref, a0_ref), (wa1_ref, a1_ref), (wa2_ref, a2_ref),
                         (wfq_ref, fq_ref), (wfk_ref, fk_ref), (wfv_ref, fv_ref)):
        o_ref[0] = _dot(h, w_ref[...]).astype(BF16)
    gate_ref[0] = jax.nn.sigmoid(_dot(h, wg_ref[...]) + bg_ref[...]).astype(BF16)

    z = _dot(h, wf_ref[...]) + bf_ref[...]
    logf = jnp.minimum(z, 0.0) - jnp.log1p(jnp.exp(-jnp.abs(z)))
    hi = logf.astype(BF16)
    r1 = logf - hi.astype(F32)
    mid = r1.astype(BF16)
    lo = (r1 - mid.astype(F32)).astype(BF16)
    row = lax.broadcasted_iota(jnp.int32, (tm, tm), 0)
    col = lax.broadcasted_iota(jnp.int32, (tm, tm), 1)
    tril = jnp.where(col <= row, 1.0, 0.0).astype(BF16)
    c = _dot(tril, hi) + _dot(tril, mid) + _dot(tril, lo) + carry_ref[...]
    carry_ref[...] = c[tm - 1:tm, :]
    c_ref[0] = c[:, :N_FOX_HEADS]


def _inproj(x, norm_g, w_in, b_fgt, b_gate, tm):
    b, s, _ = x.shape
    o2 = 3 * DIL_WIDTH
    o3 = o2 + 3 * FOX_WIDTH
    o4 = o3 + N_FOX_HEADS
    wq = w_in[:, :DIL_WIDTH] * ATTN_SCALE
    wk = w_in[:, DIL_WIDTH:2 * DIL_WIDTH]
    wv = w_in[:, 2 * DIL_WIDTH:o2]
    w = DIL_OUT_WIDTH
    wa = [jnp.concatenate([wq[:, g * w:(g + 1) * w], wk[:, g * w:(g + 1) * w],
                           wv[:, g * w:(g + 1) * w]], axis=1).astype(BF16)
          for g in range(N_DIL_GROUPS)]
    wfq = (w_in[:, o2:o2 + FOX_WIDTH] * ATTN_SCALE).astype(BF16)
    wfk = w_in[:, o2 + FOX_WIDTH:o2 + 2 * FOX_WIDTH].astype(BF16)
    wfv = w_in[:, o2 + 2 * FOX_WIDTH:o3].astype(BF16)
    wf = jnp.pad(w_in[:, o3:o4], ((0, 0), (0, LANES - N_FOX_HEADS))).astype(BF16)
    bf = jnp.pad(b_fgt, (0, LANES - N_FOX_HEADS)).reshape(1, LANES)
    wg = w_in[:, o4:].astype(BF16)
    bg = b_gate.reshape(1, -1)

    def const(shape):
        return pl.BlockSpec(shape, lambda bi, si: (0,) * len(shape))

    def tok(width):
        return pl.BlockSpec((1, tm, width), lambda bi, si: (bi, si, 0))

    widths = (3 * w, 3 * w, 3 * w, FOX_WIDTH, FOX_WIDTH, FOX_WIDTH)
    out_shape = [jax.ShapeDtypeStruct((b, s, wd), BF16) for wd in widths]
    out_shape += [jax.ShapeDtypeStruct((b, s, N_FOX_HEADS), F32),
                  jax.ShapeDtypeStruct((b, s, 2 * D_MODEL), BF16)]
    out_specs = [tok(wd) for wd in widths] + [tok(N_FOX_HEADS), tok(2 * D_MODEL)]
    in_specs = [tok(D_MODEL), const((1, D_MODEL))]
    in_specs += [const((D_MODEL, wd)) for wd in widths]
    in_specs += [const((D_MODEL, LANES)), const((1, LANES)),
                 const((D_MODEL, 2 * D_MODEL)), const((1, 2 * D_MODEL))]
    return pl.pallas_call(
        _inproj_kernel,
        grid=(b, s // tm),
        in_specs=in_specs,
        out_specs=out_specs,
        out_shape=out_shape,
        scratch_shapes=[pltpu.VMEM((1, LANES), F32)],
        compiler_params=pltpu.CompilerParams(
            dimension_semantics=("arbitrary", "arbitrary"), vmem_limit_bytes=VMEM_LIMIT),
        name="inproj",
    )(x, norm_g.reshape(1, -1), *wa, wfq, wfk, wfv, wf, bf, wg, bg)


def _dil_kernel(cur_ref, prev_ref, o_ref, lse_ref, *, dilation, slopes):
    w = DIL_OUT_WIDTH
    n = pl.program_id(1)
    blk = DIL_BLOCK
    row = lax.broadcasted_iota(jnp.int32, (blk, blk), 0)
    col = lax.broadcasted_iota(jnp.int32, (blk, blk), 1)
    rel_c = row - col
    valid_c = rel_c >= 0
    valid_p = jnp.logical_and(rel_c <= 0, n > 0)
    dist_c = (rel_c * dilation).astype(F32)
    dist_p = ((rel_c + blk) * dilation).astype(F32)
    lane = lax.broadcasted_iota(jnp.int32, (blk, LANES), 1)
    low = lane < HEAD_DIM

    for hp in range(HEADS_PER_DIL_GROUP // 2):
        lo_, hi_ = hp * LANES, (hp + 1) * LANES
        q_pair = cur_ref[0, :, lo_:hi_]
        k_c = cur_ref[0, :, w + lo_:w + hi_]
        v_c = cur_ref[0, :, 2 * w + lo_:2 * w + hi_]
        k_p = prev_ref[0, :, w + lo_:w + hi_]
        v_p = prev_ref[0, :, 2 * w + lo_:2 * w + hi_]
        outs, lses = [], []
        for j in range(2):
            slope = slopes[2 * hp + j]
            q_h = jnp.where(low if j == 0 else jnp.logical_not(low), q_pair, jnp.zeros_like(q_pair))
            s_c = jnp.where(valid_c, _dot_nt(q_h, k_c) - slope * dist_c, NEG_INF)
            s_p = jnp.where(valid_p, _dot_nt(q_h, k_p) - slope * dist_p, NEG_INF)
            m = jnp.maximum(jnp.max(s_c, axis=-1, keepdims=True),
                            jnp.max(s_p, axis=-1, keepdims=True))
            p_c = jnp.exp(s_c - m)
            p_p = jnp.exp(s_p - m)
            den = jnp.sum(p_c, axis=-1, keepdims=True) + jnp.sum(p_p, axis=-1, keepdims=True)
            acc = _dot(p_c.astype(BF16), v_c) + _dot(p_p.astype(BF16), v_p)
            outs.append(acc / den)
            lses.append(jnp.broadcast_to(m + jnp.log(den), (blk, LANES)))
        o_ref[0, :, lo_:hi_] = jnp.where(low, outs[0], outs[1]).astype(o_ref.dtype)
        lse_ref[0, :, lo_:hi_] = jnp.where(low, lses[0], lses[1])


def _dilated(a_g, group, batch, seq):
    window, dilation = DIL_PAIRS[group]
    assert window // dilation == DIL_BLOCK
    w = DIL_OUT_WIDTH
    rows = seq // dilation
    nblk = rows // DIL_BLOCK
    slopes = tuple(float(2.0 ** (-8.0 * (group * HEADS_PER_DIL_GROUP + h + 1) / N_DIL_HEADS))
                   for h in range(HEADS_PER_DIL_GROUP))
    view = a_g.reshape(batch, rows, dilation * 3 * w)
    o, lse = pl.pallas_call(
        functools.partial(_dil_kernel, dilation=dilation, slopes=slopes),
        grid=(batch, nblk, dilation),
        in_specs=[pl.BlockSpec((1, DIL_BLOCK, 3 * w), lambda b, n, r: (b, n, r)),
                  pl.BlockSpec((1, DIL_BLOCK, 3 * w), lambda b, n, r: (b, jnp.maximum(n - 1, 0), r))],
        out_specs=[pl.BlockSpec((1, DIL_BLOCK, w), lambda b, n, r: (b, n, r)),
                   pl.BlockSpec((1, DIL_BLOCK, w), lambda b, n, r: (b, n, r))],
        out_shape=[jax.ShapeDtypeStruct((batch, rows, dilation * w), BF16),
                   jax.ShapeDtypeStruct((batch, rows, dilation * w), F32)],
        compiler_params=pltpu.CompilerParams(
            dimension_semantics=("arbitrary", "arbitrary", "arbitrary")),
        name=f"dilated{group}",
    )(view, view)
    return o.reshape(batch * seq, w), lse.reshape(batch * seq, w)


def _fox_kernel(q_ref, k_ref, v_ref, cq_ref, ct_ref, o_ref, *, tq):
    hp = pl.program_id(1)
    i = pl.program_id(2)
    q_pair = q_ref[0]
    lane = lax.broadcasted_iota(jnp.int32, (tq, LANES), 1)
    low = lane < HEAD_DIM
    head_lane = lax.broadcasted_iota(jnp.int32, (tq, N_FOX_HEADS), 1)
    row = lax.broadcasted_iota(jnp.int32, (tq, tq), 0)
    col = lax.broadcasted_iota(jnp.int32, (tq, tq), 1)
    causal = col <= row
    c_blk = cq_ref[0]

    outs = []
    for j in range(2):
        h = 2 * hp + j
        q_h = jnp.where(low if j == 0 else jnp.logical_not(low), q_pair, jnp.zeros_like(q_pair))
        c_q = jnp.sum(jnp.where(head_lane == h, c_blk, 0.0), axis=-1, keepdims=True)

        def step(kt, carry, masked):
            m, l, acc = carry
            ks = pl.multiple_of(kt * tq, tq)
            k_t = k_ref[0, pl.ds(ks, tq), :]
            v_t = v_ref[0, pl.ds(ks, tq), :]
            c_k = ct_ref[0, pl.ds(h, 1), pl.ds(ks, tq)]
            s = _dot_nt(q_h, k_t) + (c_q - c_k)
            if masked:
                s = jnp.where(causal, s, NEG_INF)
            m_new = jnp.maximum(m, jnp.max(s, axis=-1, keepdims=True))
            alpha = jnp.exp(m - m_new)
            p = jnp.exp(s - m_new)
            l = alpha * l + jnp.sum(p, axis=-1, keepdims=True)
            acc = alpha * acc + _dot(p.astype(BF16), v_t)
            return m_new, l, acc

        init = (jnp.full((tq, 1), NEG_INF, F32), jnp.zeros((tq, 1), F32),
                jnp.zeros((tq, LANES), F32))
        carry = lax.fori_loop(0, i, functools.partial(step, masked=False), init)
        _, l, acc = step(i, carry, True)
        outs.append(acc / l)
    o_ref[0] = jnp.where(low, outs[0], outs[1]).astype(o_ref.dtype)


def _fox(fq, fk, fv, c, c_t, tq):
    b, s, _ = fq.shape
    return pl.pallas_call(
        functools.partial(_fox_kernel, tq=tq),
        grid=(b, N_FOX_HEADS // 2, s // tq),
        in_specs=[pl.BlockSpec((1, tq, LANES), lambda bi, hp, i: (bi, i, hp)),
                  pl.BlockSpec((1, s, LANES), lambda bi, hp, i: (bi, 0, hp)),
                  pl.BlockSpec((1, s, LANES), lambda bi, hp, i: (bi, 0, hp)),
                  pl.BlockSpec((1, tq, N_FOX_HEADS), lambda bi, hp, i: (bi, i, 0)),
                  pl.BlockSpec((1, N_FOX_HEADS, s), lambda bi, hp, i: (bi, 0, 0))],
        out_specs=pl.BlockSpec((1, tq, LANES), lambda bi, hp, i: (bi, i, hp)),
        out_shape=jax.ShapeDtypeStruct((b, s, FOX_WIDTH), BF16),
        compiler_params=pltpu.CompilerParams(
            dimension_semantics=("arbitrary", "arbitrary", "arbitrary")),
        name="fox",
    )(fq, fk, fv, c, c_t)


FF_CHUNK = 256


def _mix_ffn_kernel(x_ref, o0_ref, o1_ref, o2_ref, l0_ref, l1_ref, l2_ref, ob_ref, gate_ref,
                    wd_ref, wfo_ref, wo_ref, g2_ref, wgi_ref, wui_ref, wdn_ref, gfin_ref,
                    out_ref):
    l0, l1, l2 = l0_ref[...], l1_ref[...], l2_ref[...]
    m = jnp.maximum(jnp.maximum(l0, l1), l2)
    e0, e1, e2 = jnp.exp(l0 - m), jnp.exp(l1 - m), jnp.exp(l2 - m)
    o_a = (e0 * o0_ref[...].astype(F32) + e1 * o1_ref[...].astype(F32)
           + e2 * o2_ref[...].astype(F32)) / (e0 + e1 + e2)
    y_a = _dot(o_a.astype(BF16), wd_ref[...])
    y_b = _dot(ob_ref[...], wfo_ref[...])
    merged = (gate_ref[:, :D_MODEL].astype(F32) * y_a
              + gate_ref[:, D_MODEL:].astype(F32) * y_b)
    x1 = x_ref[...] + _dot(merged.astype(BF16), wo_ref[...])

    h2 = _rms(x1, g2_ref[...]).astype(BF16)
    acc = jnp.zeros_like(x1)
    for c0 in range(0, D_FF, FF_CHUNK):
        gte = _dot(h2, wgi_ref[:, c0:c0 + FF_CHUNK])
        up = _dot(h2, wui_ref[:, c0:c0 + FF_CHUNK])
        act = gte * jax.nn.sigmoid(gte) * up
        acc = acc + _dot(act.astype(BF16), wdn_ref[c0:c0 + FF_CHUNK, :])
    out_ref[...] = _rms(x1 + acc, gfin_ref[...])


def _mix_ffn(x2d, o_g, lse_g, o_b, gates, w_dil_out, w_fox_out, w_out, norm_ffn_g,
             w_ffn_in, w_ffn_down, norm_final_g, tm):
    n = x2d.shape[0]
    assert D_FF % FF_CHUNK == 0

    def const(shape):
        return pl.BlockSpec(shape, lambda i: (0,) * len(shape), pipeline_mode=pl.Buffered(1))

    def tok(width):
        return pl.BlockSpec((tm, width), lambda i: (i, 0))

    in_specs = [tok(D_MODEL)] + [tok(DIL_OUT_WIDTH)] * 6 + [tok(FOX_WIDTH), tok(2 * D_MODEL)]
    in_specs += [const((DIL_OUT_WIDTH, D_MODEL)), const((FOX_WIDTH, D_MODEL)),
                 const((D_MODEL, D_MODEL)), const((1, D_MODEL)),
                 const((D_MODEL, D_FF)), const((D_MODEL, D_FF)), const((D_FF, D_MODEL)),
                 const((1, D_MODEL))]
    return pl.pallas_call(
        _mix_ffn_kernel,
        grid=(n // tm,),
        in_specs=in_specs,
        out_specs=tok(D_MODEL),
        out_shape=jax.ShapeDtypeStruct((n, D_MODEL), F32),
        compiler_params=pltpu.CompilerParams(
            dimension_semantics=("arbitrary",), vmem_limit_bytes=VMEM_LIMIT),
        name="mix_ffn",
    )(x2d, *o_g, *lse_g, o_b, gates,
      w_dil_out.astype(BF16), w_fox_out.astype(BF16), w_out.astype(BF16),
      norm_ffn_g.reshape(1, -1),
      w_ffn_in[:, :D_FF].astype(BF16), w_ffn_in[:, D_FF:].astype(BF16),
      w_ffn_down.astype(BF16), norm_final_g.reshape(1, -1))


def kernel(x, norm_mix_g, w_in, b_fgt, b_gate, w_dil_out, w_fox_out, w_out, norm_ffn_g,
           w_ffn_in, w_ffn_down, norm_final_g):
    b, s, d = x.shape
    depth = w_in.shape[0]
    for layer in range(depth):
        a0, a1, a2, fq, fk, fv, c, gates = _inproj(
            x, norm_mix_g[layer], w_in[layer], b_fgt[layer], b_gate[layer], tm=512)
        o_g, lse_g = zip(*[_dilated(a, g, b, s) for g, a in enumerate((a0, a1, a2))])
        o_b = _fox(fq, fk, fv, c, jnp.swapaxes(c, 1, 2), tq=256)
        assert layer == depth - 1
        x = _mix_ffn(x.reshape(b * s, d), o_g, lse_g, o_b.reshape(b * s, FOX_WIDTH),
                     gates.reshape(b * s, 2 * D_MODEL), w_dil_out[layer], w_fox_out[layer],
                     w_out[layer], norm_ffn_g[layer], w_ffn_in[layer], w_ffn_down[layer],
                     norm_final_g, tm=256).reshape(b, s, d)
    return x
```

```python
import functools

import jax
import jax.numpy as jnp
from jax import lax
from jax.experimental import pallas as pl
from jax.experimental.pallas import tpu as pltpu

D_MODEL = 1024
HEAD_DIM = 64
DIL_PAIRS = ((128, 1), (512, 4), (2048, 16))
N_DIL_GROUPS = len(DIL_PAIRS)
HEADS_PER_DIL_GROUP = 4
N_DIL_HEADS = N_DIL_GROUPS * HEADS_PER_DIL_GROUP
DIL_OUT_WIDTH = HEADS_PER_DIL_GROUP * HEAD_DIM
DIL_WIDTH = N_DIL_HEADS * HEAD_DIM
N_FOX_HEADS = 8
FOX_WIDTH = N_FOX_HEADS * HEAD_DIM
D_FF = 2816
RMS_EPS = 1e-6
NEG_INF = -1e30
ATTN_SCALE = HEAD_DIM ** -0.5
LOG2E = 1.4426950408889634

LANES = 128
DIL_BLOCK = 128
VMEM_LIMIT = 56 * 1024 * 1024

F32 = jnp.float32
BF16 = jnp.bfloat16


def _dot(a, b):
    return jnp.dot(a, b, preferred_element_type=F32)


def _dot_nt(a, b):
    return lax.dot_general(a, b, (((1,), (1,)), ((), ())), preferred_element_type=F32)


def _rms(x, g):
    ms = jnp.mean(x * x, axis=-1, keepdims=True)
    return x * lax.rsqrt(ms + RMS_EPS) * g


def _inproj_kernel(x_ref, g_ref, wa0_ref, wa1_ref, wa2_ref, wfq_ref, wfk_ref, wfv_ref,
                   wf_ref, bf_ref, wg_ref, bg_ref,
                   a0_ref, a1_ref, a2_ref, fq_ref, fk_ref, fv_ref, c_ref, gate_ref,
                   carry_ref):
    tm = x_ref.shape[1]

    @pl.when(pl.program_id(1) == 0)
    def _():
        carry_ref[...] = jnp.zeros_like(carry_ref)

    h = _rms(x_ref[0], g_ref[...]).astype(BF16)
    for w_ref, o_ref in ((wa0_ref, a0_ref), (wa1_ref, a1_ref), (wa2_ref, a2_ref),
                         (wfq_ref, fq_ref), (wfk_ref, fk_ref), (wfv_ref, fv_ref)):
        o_ref[0] = _dot(h, w_ref[...]).astype(BF16)
    gate_ref[0] = jax.nn.sigmoid(_dot(h, wg_ref[...]) + bg_ref[...]).astype(BF16)

    z = _dot(h, wf_ref[...]) + bf_ref[...]
    logf = (jnp.minimum(z, 0.0) - jnp.log1p(jnp.exp(-jnp.abs(z)))) * LOG2E
    hi = logf.astype(BF16)
    r1 = logf - hi.astype(F32)
    mid = r1.astype(BF16)
    lo = (r1 - mid.astype(F32)).astype(BF16)
    row = lax.broadcasted_iota(jnp.int32, (tm, tm), 0)
    col = lax.broadcasted_iota(jnp.int32, (tm, tm), 1)
    tril = jnp.where(col <= row, 1.0, 0.0).astype(BF16)
    c = _dot(tril, hi) + _dot(tril, mid) + _dot(tril, lo) + carry_ref[...]
    carry_ref[...] = c[tm - 1:tm, :]
    c_ref[0] = c[:, :N_FOX_HEADS]


def _inproj(x, norm_g, w_in, b_fgt, b_gate, tm):
    b, s, _ = x.shape
    o2 = 3 * DIL_WIDTH
    o3 = o2 + 3 * FOX_WIDTH
    o4 = o3 + N_FOX_HEADS
    wq = w_in[:, :DIL_WIDTH] * ATTN_SCALE
    wk = w_in[:, DIL_WIDTH:2 * DIL_WIDTH]
    wv = w_in[:, 2 * DIL_WIDTH:o2]
    w = DIL_OUT_WIDTH
    wa = [jnp.concatenate([wq[:, g * w:(g + 1) * w], wk[:, g * w:(g + 1) * w],
                           wv[:, g * w:(g + 1) * w]], axis=1).astype(BF16)
          for g in range(N_DIL_GROUPS)]
    wfq = (w_in[:, o2:o2 + FOX_WIDTH] * (ATTN_SCALE * LOG2E)).astype(BF16)
    wfk = w_in[:, o2 + FOX_WIDTH:o2 + 2 * FOX_WIDTH].astype(BF16)
    wfv = w_in[:, o2 + 2 * FOX_WIDTH:o3].astype(BF16)
    wf = jnp.pad(w_in[:, o3:o4], ((0, 0), (0, LANES - N_FOX_HEADS))).astype(BF16)
    bf = jnp.pad(b_fgt, (0, LANES - N_FOX_HEADS)).reshape(1, LANES)
    wg = w_in[:, o4:].astype(BF16)
    bg = b_gate.reshape(1, -1)

    def const(shape):
        return pl.BlockSpec(shape, lambda bi, si: (0,) * len(shape))

    def tok(width):
        return pl.BlockSpec((1, tm, width), lambda bi, si: (bi, si, 0))

    widths = (3 * w, 3 * w, 3 * w, FOX_WIDTH, FOX_WIDTH, FOX_WIDTH)
    out_shape = [jax.ShapeDtypeStruct((b, s, wd), BF16) for wd in widths]
    out_shape += [jax.ShapeDtypeStruct((b, s, N_FOX_HEADS), F32),
                  jax.ShapeDtypeStruct((b, s, 2 * D_MODEL), BF16)]
    out_specs = [tok(wd) for wd in widths] + [tok(N_FOX_HEADS), tok(2 * D_MODEL)]
    in_specs = [tok(D_MODEL), const((1, D_MODEL))]
    in_specs += [const((D_MODEL, wd)) for wd in widths]
    in_specs += [const((D_MODEL, LANES)), const((1, LANES)),
                 const((D_MODEL, 2 * D_MODEL)), const((1, 2 * D_MODEL))]
    return pl.pallas_call(
        _inproj_kernel,
        grid=(b, s // tm),
        in_specs=in_specs,
        out_specs=out_specs,
        out_shape=out_shape,
        scratch_shapes=[pltpu.VMEM((1, LANES), F32)],
        compiler_params=pltpu.CompilerParams(
            dimension_semantics=("arbitrary", "arbitrary"), vmem_limit_bytes=VMEM_LIMIT),
        name="inproj",
    )(x, norm_g.reshape(1, -1), *wa, wfq, wfk, wfv, wf, bf, wg, bg)


def _dil_kernel(cur_ref, prev_ref, o_ref, lse_ref, *, dilation, slopes, nsub):
    w = DIL_OUT_WIDTH
    blk = DIL_BLOCK
    n = pl.program_id(1)
    row = lax.broadcasted_iota(jnp.int32, (blk, 2 * blk), 0)
    col = lax.broadcasted_iota(jnp.int32, (blk, 2 * blk), 1)
    rel = row - col + blk
    valid = jnp.logical_and(rel >= 0, rel <= blk)
    valid_first = jnp.logical_and(valid, jnp.logical_or(col >= blk, n > 0))
    dist = (rel * dilation).astype(F32)
    bias = [slope * dist for slope in slopes]
    lane = lax.broadcasted_iota(jnp.int32, (blk, LANES), 1)
    low = lane < HEAD_DIM

    for sb in range(nsub):
        r0 = sb * blk
        for hp in range(HEADS_PER_DIL_GROUP // 2):
            lo_, hi_ = hp * LANES, (hp + 1) * LANES
            q_pair = cur_ref[0, r0:r0 + blk, lo_:hi_]
            q2 = jnp.concatenate([jnp.where(low, q_pair, jnp.zeros_like(q_pair)),
                                  jnp.where(low, jnp.zeros_like(q_pair), q_pair)], axis=0)
            if sb == 0:
                k2 = jnp.concatenate([prev_ref[0, :, w + lo_:w + hi_],
                                      cur_ref[0, :blk, w + lo_:w + hi_]], axis=0)
                v2 = jnp.concatenate([prev_ref[0, :, 2 * w + lo_:2 * w + hi_],
                                      cur_ref[0, :blk, 2 * w + lo_:2 * w + hi_]], axis=0)
                ok = valid_first
            else:
                k2 = cur_ref[0, r0 - blk:r0 + blk, w + lo_:w + hi_]
                v2 = cur_ref[0, r0 - blk:r0 + blk, 2 * w + lo_:2 * w + hi_]
                ok = valid
            s2 = _dot_nt(q2, k2)
            ps, dens, lses = [], [], []
            for j in range(2):
                s = jnp.where(ok, s2[j * blk:(j + 1) * blk] - bias[2 * hp + j], NEG_INF)
                m = jnp.max(s, axis=-1, keepdims=True)
                p = jnp.exp(s - m)
                den = jnp.sum(p, axis=-1, keepdims=True)
                ps.append(p.astype(BF16))
                dens.append(den)
                lses.append(jnp.broadcast_to(m + jnp.log(den), (blk, LANES)))
            acc2 = _dot(jnp.concatenate(ps, axis=0), v2)
            o_pair = jnp.where(low, acc2[:blk] / dens[0], acc2[blk:] / dens[1])
            o_ref[0, r0:r0 + blk, lo_:hi_] = o_pair.astype(o_ref.dtype)
            lse_ref[0, r0:r0 + blk, lo_:hi_] = jnp.where(low, lses[0], lses[1])


def _dilated(a_g, group, batch, seq):
    window, dilation = DIL_PAIRS[group]
    assert window // dilation == DIL_BLOCK
    w = DIL_OUT_WIDTH
    rows = seq // dilation
    qrows = min(rows, 4 * DIL_BLOCK)
    nsub = qrows // DIL_BLOCK
    slopes = tuple(float(2.0 ** (-8.0 * (group * HEADS_PER_DIL_GROUP + h + 1) / N_DIL_HEADS))
                   for h in range(HEADS_PER_DIL_GROUP))
    view = a_g.reshape(batch, rows, dilation * 3 * w)
    o, lse = pl.pallas_call(
        functools.partial(_dil_kernel, dilation=dilation, slopes=slopes, nsub=nsub),
        grid=(batch, rows // qrows, dilation),
        in_specs=[pl.BlockSpec((1, qrows, 3 * w), lambda b, n, r: (b, n, r)),
                  pl.BlockSpec((1, DIL_BLOCK, 3 * w),
                               lambda b, n, r: (b, jnp.maximum(n * nsub - 1, 0), r))],
        out_specs=[pl.BlockSpec((1, qrows, w), lambda b, n, r: (b, n, r)),
                   pl.BlockSpec((1, qrows, w), lambda b, n, r: (b, n, r))],
        out_shape=[jax.ShapeDtypeStruct((batch, rows, dilation * w), BF16),
                   jax.ShapeDtypeStruct((batch, rows, dilation * w), F32)],
        compiler_params=pltpu.CompilerParams(
            dimension_semantics=("arbitrary", "arbitrary", "arbitrary")),
        name=f"dilated{group}",
    )(view, view)
    return o.reshape(batch * seq, w), lse.reshape(batch * seq, w)


def _fox_kernel(q_ref, k_ref, v_ref, cq_ref, ct_ref, o_ref,
                s_ref, mx_ref, m_ref, l_ref, acc_ref, *, tq):
    hp = pl.program_id(1)
    i = pl.program_id(2)
    q_pair = q_ref[0]
    lane = lax.broadcasted_iota(jnp.int32, (tq, LANES), 1)
    low = lane < HEAD_DIM
    q2 = jnp.concatenate([jnp.where(low, q_pair, jnp.zeros_like(q_pair)),
                          jnp.where(low, jnp.zeros_like(q_pair), q_pair)], axis=0)
    head_lane = lax.broadcasted_iota(jnp.int32, (tq, N_FOX_HEADS), 1)
    c_blk = cq_ref[0]
    c_q = [jnp.tile(jnp.broadcast_to(
        jnp.sum(jnp.where(head_lane == 2 * hp + j, c_blk, 0.0), axis=-1, keepdims=True),
        (tq, LANES)), (1, tq // LANES)) for j in range(2)]
    row = lax.broadcasted_iota(jnp.int32, (tq, tq), 0)
    col = lax.broadcasted_iota(jnp.int32, (tq, tq), 1)
    causal = col <= row

    def produce(kt, slot, masked):
        ks = pl.multiple_of(kt * tq, tq)
        s2 = _dot_nt(q2, k_ref[0, pl.ds(ks, tq), :])
        for j in range(2):
            c_k = ct_ref[0, pl.ds(2 * hp + j, 1), pl.ds(ks, tq)]
            s = (s2[j * tq:(j + 1) * tq] + c_q[j]) - c_k
            if masked:
                s = jnp.where(causal, s, NEG_INF)
            s_ref[slot, j * tq:(j + 1) * tq, :] = s
            mx_ref[slot, j * tq:(j + 1) * tq, :] = jnp.broadcast_to(
                jnp.max(s, axis=-1, keepdims=True), (tq, LANES))

    def consume(kt, slot):
        ks = pl.multiple_of(kt * tq, tq)
        m = m_ref[...]
        m_new = jnp.maximum(m, mx_ref[slot])
        alpha = jnp.exp2(m - m_new)
        p = jnp.exp2(s_ref[slot] - jnp.tile(m_new, (1, tq // LANES)))
        l_ref[...] = alpha * l_ref[...] + jnp.broadcast_to(
            jnp.sum(p, axis=-1, keepdims=True), (2 * tq, LANES))
        acc_ref[...] = alpha * acc_ref[...] + _dot(p.astype(BF16), v_ref[0, pl.ds(ks, tq), :])
        m_ref[...] = m_new

    def body(kt, kt_prev):
        consume(kt_prev, kt & 1)
        produce(kt, (kt + 1) & 1, False)
        return kt

    m_ref[...] = jnp.full_like(m_ref, NEG_INF)
    l_ref[...] = jnp.zeros_like(l_ref)
    acc_ref[...] = jnp.zeros_like(acc_ref)
    produce(i, 0, True)
    kt_last = lax.fori_loop(0, i, body, i)
    consume(kt_last, i & 1)
    out = acc_ref[...] / l_ref[...]
    o_ref[0] = jnp.where(low, out[:tq], out[tq:]).astype(o_ref.dtype)


def _fox(fq, fk, fv, c, c_t, tq):
    b, s, _ = fq.shape
    return pl.pallas_call(
        functools.partial(_fox_kernel, tq=tq),
        grid=(b, N_FOX_HEADS // 2, s // tq),
        in_specs=[pl.BlockSpec((1, tq, LANES), lambda bi, hp, i: (bi, i, hp)),
                  pl.BlockSpec((1, s, LANES), lambda bi, hp, i: (bi, 0, hp)),
                  pl.BlockSpec((1, s, LANES), lambda bi, hp, i: (bi, 0, hp)),
                  pl.BlockSpec((1, tq, N_FOX_HEADS), lambda bi, hp, i: (bi, i, 0)),
                  pl.BlockSpec((1, N_FOX_HEADS, s), lambda bi, hp, i: (bi, 0, 0))],
        out_specs=pl.BlockSpec((1, tq, LANES), lambda bi, hp, i: (bi, i, hp)),
        out_shape=jax.ShapeDtypeStruct((b, s, FOX_WIDTH), BF16),
        scratch_shapes=[pltpu.VMEM((2, 2 * tq, tq), F32), pltpu.VMEM((2, 2 * tq, LANES), F32),
                        pltpu.VMEM((2 * tq, LANES), F32), pltpu.VMEM((2 * tq, LANES), F32),
                        pltpu.VMEM((2 * tq, LANES), F32)],
        compiler_params=pltpu.CompilerParams(
            dimension_semantics=("arbitrary", "arbitrary", "arbitrary"),
            vmem_limit_bytes=VMEM_LIMIT),
        name="fox",
    )(fq, fk, fv, c, c_t)


FF_CHUNK = 256


def _mix_ffn_kernel(x_ref, o0_ref, o1_ref, o2_ref, l0_ref, l1_ref, l2_ref, ob_ref, gate_ref,
                    wd_ref, wfo_ref, wo_ref, g2_ref, wgi_ref, wui_ref, wdn_ref, gfin_ref,
                    out_ref):
    l0, l1, l2 = l0_ref[...], l1_ref[...], l2_ref[...]
    m = jnp.maximum(jnp.maximum(l0, l1), l2)
    e0, e1, e2 = jnp.exp(l0 - m), jnp.exp(l1 - m), jnp.exp(l2 - m)
    o_a = (e0 * o0_ref[...].astype(F32) + e1 * o1_ref[...].astype(F32)
           + e2 * o2_ref[...].astype(F32)) / (e0 + e1 + e2)
    y_a = _dot(o_a.astype(BF16), wd_ref[...])
    y_b = _dot(ob_ref[...], wfo_ref[...])
    merged = (gate_ref[:, :D_MODEL].astype(F32) * y_a
              + gate_ref[:, D_MODEL:].astype(F32) * y_b)
    x1 = x_ref[...] + _dot(merged.astype(BF16), wo_ref[...])

    h2 = _rms(x1, g2_ref[...]).astype(BF16)
    acc = jnp.zeros_like(x1)
    for c0 in range(0, D_FF, FF_CHUNK):
        gte = _dot(h2, wgi_ref[:, c0:c0 + FF_CHUNK])
        up = _dot(h2, wui_ref[:, c0:c0 + FF_CHUNK])
        act = gte * jax.nn.sigmoid(gte) * up
        acc = acc + _dot(act.astype(BF16), wdn_ref[c0:c0 + FF_CHUNK, :])
    out_ref[...] = _rms(x1 + acc, gfin_ref[...])


def _mix_ffn(x2d, o_g, lse_g, o_b, gates, w_dil_out, w_fox_out, w_out, norm_ffn_g,
             w_ffn_in, w_ffn_down, norm_final_g, tm):
    n = x2d.shape[0]
    assert D_FF % FF_CHUNK == 0

    def const(shape):
        return pl.BlockSpec(shape, lambda i: (0,) * len(shape), pipeline_mode=pl.Buffered(1))

    def tok(width):
        return pl.BlockSpec((tm, width), lambda i: (i, 0))

    in_specs = [tok(D_MODEL)] + [tok(DIL_OUT_WIDTH)] * 6 + [tok(FOX_WIDTH), tok(2 * D_MODEL)]
    in_specs += [const((DIL_OUT_WIDTH, D_MODEL)), const((FOX_WIDTH, D_MODEL)),
                 const((D_MODEL, D_MODEL)), const((1, D_MODEL)),
                 const((D_MODEL, D_FF)), const((D_MODEL, D_FF)), const((D_FF, D_MODEL)),
                 const((1, D_MODEL))]
    return pl.pallas_call(
        _mix_ffn_kernel,
        grid=(n // tm,),
        in_specs=in_specs,
        out_specs=tok(D_MODEL),
        out_shape=jax.ShapeDtypeStruct((n, D_MODEL), F32),
        compiler_params=pltpu.CompilerParams(
            dimension_semantics=("arbitrary",), vmem_limit_bytes=VMEM_LIMIT),
        name="mix_ffn",
    )(x2d, *o_g, *lse_g, o_b, gates,
      w_dil_out.astype(BF16), w_fox_out.astype(BF16), w_out.astype(BF16),
      norm_ffn_g.reshape(1, -1),
      w_ffn_in[:, :D_FF].astype(BF16), w_ffn_in[:, D_FF:].astype(BF16),
      w_ffn_down.astype(BF16), norm_final_g.reshape(1, -1))


def kernel(x, norm_mix_g, w_in, b_fgt, b_gate, w_dil_out, w_fox_out, w_out, norm_ffn_g,
           w_ffn_in, w_ffn_down, norm_final_g):
    b, s, d = x.shape
    assert w_in.shape[0] == 1
    layer = 0
    a0, a1, a2, fq, fk, fv, c, gates = _inproj(
        x, norm_mix_g[layer], w_in[layer], b_fgt[layer], b_gate[layer], tm=512)
    o_g, lse_g = zip(*[_dilated(a, g, b, s) for g, a in enumerate((a0, a1, a2))])
    o_b = _fox(fq, fk, fv, c, jnp.swapaxes(c, 1, 2), tq=512)
    out = _mix_ffn(x.reshape(b * s, d), o_g, lse_g, o_b.reshape(b * s, FOX_WIDTH),
                   gates.reshape(b * s, 2 * D_MODEL), w_dil_out[layer], w_fox_out[layer],
                   w_out[layer], norm_ffn_g[layer], w_ffn_in[layer], w_ffn_down[layer],
                   norm_final_g, tm=256)
    return out.reshape(b, s, d)
```

```python
import functools

import jax
import jax.numpy as jnp
from jax import lax
from jax.experimental import pallas as pl
from jax.experimental.pallas import tpu as pltpu

D_MODEL = 1024
HEAD_DIM = 64
DIL_PAIRS = ((128, 1), (512, 4), (2048, 16))
N_DIL_GROUPS = len(DIL_PAIRS)
HEADS_PER_DIL_GROUP = 4
N_DIL_HEADS = N_DIL_GROUPS * HEADS_PER_DIL_GROUP
DIL_OUT_WIDTH = HEADS_PER_DIL_GROUP * HEAD_DIM
DIL_QKV_WIDTH = 3 * DIL_OUT_WIDTH
DIL_WIDTH = N_DIL_HEADS * HEAD_DIM
N_FOX_HEADS = 8
FOX_WIDTH = N_FOX_HEADS * HEAD_DIM
D_FF = 2816
RMS_EPS = 1e-6
NEG_INF = -1e30
ATTN_SCALE = HEAD_DIM ** -0.5
LOG2E = 1.4426950408889634

LANES = 128
DIL_BLOCK = 128
VMEM_LIMIT = 56 * 1024 * 1024

F32 = jnp.float32
BF16 = jnp.bfloat16


def _dot(a, b):
    return jnp.dot(a, b, preferred_element_type=F32)


def _dot_nt(a, b):
    return lax.dot_general(a, b, (((1,), (1,)), ((), ())), preferred_element_type=F32)


def _rms(x, g):
    ms = jnp.mean(x * x, axis=-1, keepdims=True)
    return x * lax.rsqrt(ms + RMS_EPS) * g


def _resident(shape, index_map):
    return pl.BlockSpec(shape, index_map, pipeline_mode=pl.Buffered(1))


def _inproj_kernel(x_ref, g_ref, wa0_ref, wa1_ref, wa2_ref, wfq_ref, wfk_ref, wfv_ref,
                   wf_ref, bf_ref, wg_ref, bg_ref,
                   a0_ref, a1_ref, a2_ref, fq_ref, fk_ref, fv_ref, c_ref, gate_ref,
                   carry_ref, y_ref):
    tm = x_ref.shape[1]

    @pl.when(pl.program_id(1) == 0)
    def _():
        carry_ref[...] = jnp.zeros_like(carry_ref)

    h = _rms(x_ref[0], g_ref[...]).astype(BF16)
    for (_, dilation), w_ref, o_ref in zip(DIL_PAIRS, (wa0_ref, wa1_ref, wa2_ref),
                                           (a0_ref, a1_ref, a2_ref)):
        y = _dot(h, w_ref[...])
        if dilation == 1:
            o_ref[0] = y.astype(BF16)
        else:
            for cb in range(DIL_QKV_WIDTH // LANES):
                y_ref[cb] = y[:, cb * LANES:(cb + 1) * LANES]
            for r in range(dilation):
                for cb in range(DIL_QKV_WIDTH // LANES):
                    c0 = r * DIL_QKV_WIDTH + cb * LANES
                    o_ref[0, :, c0:c0 + LANES] = (
                        y_ref[cb, pl.ds(r, tm // dilation, stride=dilation), :].astype(BF16))
    for w_ref, o_ref in ((wfq_ref, fq_ref), (wfk_ref, fk_ref), (wfv_ref, fv_ref)):
        o_ref[0] = _dot(h, w_ref[...]).astype(BF16)
    gate_ref[0] = jax.nn.sigmoid(_dot(h, wg_ref[...]) + bg_ref[...]).astype(BF16)

    z = _dot(h, wf_ref[...]) + bf_ref[...]
    logf = (jnp.minimum(z, 0.0) - jnp.log1p(jnp.exp(-jnp.abs(z)))) * LOG2E
    hi = logf.astype(BF16)
    r1 = logf - hi.astype(F32)
    mid = r1.astype(BF16)
    lo = (r1 - mid.astype(F32)).astype(BF16)
    row = lax.broadcasted_iota(jnp.int32, (tm, tm), 0)
    col = lax.broadcasted_iota(jnp.int32, (tm, tm), 1)
    tril = jnp.where(col <= row, 1.0, 0.0).astype(BF16)
    c = _dot(tril, hi) + _dot(tril, mid) + _dot(tril, lo) + carry_ref[...]
    carry_ref[...] = c[tm - 1:tm, :]
    c_ref[0] = c[:, :N_FOX_HEADS]


def _inproj(x, norm_g, w_in, b_fgt, b_gate, tm):
    b, s, _ = x.shape
    o2 = 3 * DIL_WIDTH
    o3 = o2 + 3 * FOX_WIDTH
    o4 = o3 + N_FOX_HEADS
    wq = w_in[:, :DIL_WIDTH] * ATTN_SCALE
    wk = w_in[:, DIL_WIDTH:2 * DIL_WIDTH]
    wv = w_in[:, 2 * DIL_WIDTH:o2]
    w = DIL_OUT_WIDTH
    wa = [jnp.concatenate([wq[:, g * w:(g + 1) * w], wk[:, g * w:(g + 1) * w],
                           wv[:, g * w:(g + 1) * w]], axis=1).astype(BF16)
          for g in range(N_DIL_GROUPS)]
    wfq = (w_in[:, o2:o2 + FOX_WIDTH] * (ATTN_SCALE * LOG2E)).astype(BF16)
    wfk = w_in[:, o2 + FOX_WIDTH:o2 + 2 * FOX_WIDTH].astype(BF16)
    wfv = w_in[:, o2 + 2 * FOX_WIDTH:o3].astype(BF16)
    wf = jnp.pad(w_in[:, o3:o4], ((0, 0), (0, LANES - N_FOX_HEADS))).astype(BF16)
    bf = jnp.pad(b_fgt, (0, LANES - N_FOX_HEADS)).reshape(1, LANES)
    wg = w_in[:, o4:].astype(BF16)
    bg = b_gate.reshape(1, -1)

    def const(shape):
        return _resident(shape, lambda bi, si: (0,) * len(shape))

    def tok(width, rows=tm):
        return pl.BlockSpec((1, rows, width), lambda bi, si: (bi, si, 0))

    dils = [d for _, d in DIL_PAIRS]
    assert all(tm % (16 * d) == 0 for d in dils)
    out_shape = [jax.ShapeDtypeStruct((b, s // d, d * DIL_QKV_WIDTH), BF16) for d in dils]
    out_shape += [jax.ShapeDtypeStruct((b, s, FOX_WIDTH), BF16)] * 3
    out_shape += [jax.ShapeDtypeStruct((b, s, N_FOX_HEADS), F32),
                  jax.ShapeDtypeStruct((b, s, 2 * D_MODEL), BF16)]
    out_specs = [tok(d * DIL_QKV_WIDTH, tm // d) for d in dils] + [tok(FOX_WIDTH)] * 3
    out_specs += [tok(N_FOX_HEADS), tok(2 * D_MODEL)]
    in_specs = [tok(D_MODEL), const((1, D_MODEL))]
    in_specs += [const((D_MODEL, DIL_QKV_WIDTH))] * 3 + [const((D_MODEL, FOX_WIDTH))] * 3
    in_specs += [const((D_MODEL, LANES)), const((1, LANES)),
                 const((D_MODEL, 2 * D_MODEL)), const((1, 2 * D_MODEL))]
    return pl.pallas_call(
        _inproj_kernel,
        grid=(b, s // tm),
        in_specs=in_specs,
        out_specs=out_specs,
        out_shape=out_shape,
        scratch_shapes=[pltpu.VMEM((1, LANES), F32),
                        pltpu.VMEM((DIL_QKV_WIDTH // LANES, tm, LANES), F32)],
        compiler_params=pltpu.CompilerParams(
            dimension_semantics=("arbitrary", "arbitrary"), vmem_limit_bytes=VMEM_LIMIT),
        name="inproj",
    )(x, norm_g.reshape(1, -1), *wa, wfq, wfk, wfv, wf, bf, wg, bg)


def _dil_kernel(cur_ref, prev_ref, o_ref, lse_ref, *, dilation, slopes, nsub):
    w = DIL_OUT_WIDTH
    blk = DIL_BLOCK
    n = pl.program_id(1)
    row = lax.broadcasted_iota(jnp.int32, (blk, 2 * blk), 0)
    col = lax.broadcasted_iota(jnp.int32, (blk, 2 * blk), 1)
    rel = row - col + blk
    valid = jnp.logical_and(rel >= 0, rel <= blk)
    valid_first = jnp.logical_and(valid, jnp.logical_or(col >= blk, n > 0))
    dist = (rel * dilation).astype(F32)
    bias = [slope * dist for slope in slopes]
    lane = lax.broadcasted_iota(jnp.int32, (blk, LANES), 1)
    low = lane < HEAD_DIM

    for sb in range(nsub):
        r0 = sb * blk
        for hp in range(HEADS_PER_DIL_GROUP // 2):
            lo_, hi_ = hp * LANES, (hp + 1) * LANES
            q_pair = cur_ref[0, r0:r0 + blk, lo_:hi_]
            q2 = jnp.concatenate([jnp.where(low, q_pair, jnp.zeros_like(q_pair)),
                                  jnp.where(low, jnp.zeros_like(q_pair), q_pair)], axis=0)
            if sb == 0:
                k2 = jnp.concatenate([prev_ref[0, :, w + lo_:w + hi_],
                                      cur_ref[0, :blk, w + lo_:w + hi_]], axis=0)
                v2 = jnp.concatenate([prev_ref[0, :, 2 * w + lo_:2 * w + hi_],
                                      cur_ref[0, :blk, 2 * w + lo_:2 * w + hi_]], axis=0)
                ok = valid_first
            else:
                k2 = cur_ref[0, r0 - blk:r0 + blk, w + lo_:w + hi_]
                v2 = cur_ref[0, r0 - blk:r0 + blk, 2 * w + lo_:2 * w + hi_]
                ok = valid
            s2 = _dot_nt(q2, k2)
            ps, dens, lses = [], [], []
            for j in range(2):
                s = jnp.where(ok, s2[j * blk:(j + 1) * blk] - bias[2 * hp + j], NEG_INF)
                m = jnp.max(s, axis=-1, keepdims=True)
                p = jnp.exp(s - m)
                den = jnp.sum(p, axis=-1, keepdims=True)
                ps.append(p.astype(BF16))
                dens.append(den)
                lses.append(jnp.broadcast_to(m + jnp.log(den), (blk, LANES)))
            acc2 = _dot(jnp.concatenate(ps, axis=0), v2)
            o_pair = jnp.where(low, acc2[:blk] / dens[0], acc2[blk:] / dens[1])
            o_ref[0, r0:r0 + blk, lo_:hi_] = o_pair.astype(o_ref.dtype)
            lse_ref[0, r0:r0 + blk, lo_:hi_] = jnp.where(low, lses[0], lses[1])


def _dilated(view, group):
    window, dilation = DIL_PAIRS[group]
    assert window // dilation == DIL_BLOCK
    w = DIL_OUT_WIDTH
    batch, rows, _ = view.shape
    qrows = min(rows, 4 * DIL_BLOCK)
    nsub = qrows // DIL_BLOCK
    slopes = tuple(float(2.0 ** (-8.0 * (group * HEADS_PER_DIL_GROUP + h + 1) / N_DIL_HEADS))
                   for h in range(HEADS_PER_DIL_GROUP))
    return pl.pallas_call(
        functools.partial(_dil_kernel, dilation=dilation, slopes=slopes, nsub=nsub),
        grid=(batch, rows // qrows, dilation),
        in_specs=[pl.BlockSpec((1, qrows, 3 * w), lambda b, n, r: (b, n, r)),
                  pl.BlockSpec((1, DIL_BLOCK, 3 * w),
                               lambda b, n, r: (b, jnp.maximum(n * nsub - 1, 0), r))],
        out_specs=[pl.BlockSpec((1, qrows, w), lambda b, n, r: (b, n, r)),
                   pl.BlockSpec((1, qrows, w), lambda b, n, r: (b, n, r))],
        out_shape=[jax.ShapeDtypeStruct((batch, rows, dilation * w), BF16),
                   jax.ShapeDtypeStruct((batch, rows, dilation * w), F32)],
        compiler_params=pltpu.CompilerParams(
            dimension_semantics=("arbitrary", "arbitrary", "arbitrary")),
        name=f"dilated{group}",
    )(view, view)


def _fox_kernel(q_ref, k_ref, v_ref, cq_ref, ct_ref, o_ref,
                s_ref, mx_ref, m_ref, l_ref, acc_ref, *, tq):
    hp = pl.program_id(1)
    i = pl.program_id(2)
    q_pair = q_ref[0]
    lane = lax.broadcasted_iota(jnp.int32, (tq, LANES), 1)
    low = lane < HEAD_DIM
    q2 = jnp.concatenate([jnp.where(low, q_pair, jnp.zeros_like(q_pair)),
                          jnp.where(low, jnp.zeros_like(q_pair), q_pair)], axis=0)
    head_lane = lax.broadcasted_iota(jnp.int32, (tq, N_FOX_HEADS), 1)
    c_blk = cq_ref[0]
    c_q = [jnp.tile(jnp.broadcast_to(
        jnp.sum(jnp.where(head_lane == 2 * hp + j, c_blk, 0.0), axis=-1, keepdims=True),
        (tq, LANES)), (1, tq // LANES)) for j in range(2)]
    row = lax.broadcasted_iota(jnp.int32, (tq, tq), 0)
    col = lax.broadcasted_iota(jnp.int32, (tq, tq), 1)
    causal = col <= row

    def produce(kt, slot, masked):
        ks = pl.multiple_of(kt * tq, tq)
        s2 = _dot_nt(q2, k_ref[0, pl.ds(ks, tq), :])
        for j in range(2):
            c_k = ct_ref[0, pl.ds(2 * hp + j, 1), pl.ds(ks, tq)]
            s = (s2[j * tq:(j + 1) * tq] + c_q[j]) - c_k
            if masked:
                s = jnp.where(causal, s, NEG_INF)
            s_ref[slot, j * tq:(j + 1) * tq, :] = s
            mx_ref[slot, j * tq:(j + 1) * tq, :] = jnp.broadcast_to(
                jnp.max(s, axis=-1, keepdims=True), (tq, LANES))

    def consume(kt, slot):
        ks = pl.multiple_of(kt * tq, tq)
        m = m_ref[...]
        m_new = jnp.maximum(m, mx_ref[slot])
        alpha = jnp.exp2(m - m_new)
        p = jnp.exp2(s_ref[slot] - jnp.tile(m_new, (1, tq // LANES)))
        l_ref[...] = alpha * l_ref[...] + jnp.broadcast_to(
            jnp.sum(p, axis=-1, keepdims=True), (2 * tq, LANES))
        acc_ref[...] = alpha * acc_ref[...] + _dot(p.astype(BF16), v_ref[0, pl.ds(ks, tq), :])
        m_ref[...] = m_new

    def body(kt, kt_prev):
        consume(kt_prev, kt & 1)
        produce(kt, (kt + 1) & 1, False)
        return kt

    m_ref[...] = jnp.full_like(m_ref, NEG_INF)
    l_ref[...] = jnp.zeros_like(l_ref)
    acc_ref[...] = jnp.zeros_like(acc_ref)
    produce(i, 0, True)
    kt_last = lax.fori_loop(0, i, body, i)
    consume(kt_last, i & 1)
    out = acc_ref[...] / l_ref[...]
    o_ref[0] = jnp.where(low, out[:tq], out[tq:]).astype(o_ref.dtype)


def _fox(fq, fk, fv, c, c_t, tq):
    b, s, _ = fq.shape
    return pl.pallas_call(
        functools.partial(_fox_kernel, tq=tq),
        grid=(b, N_FOX_HEADS // 2, s // tq),
        in_specs=[pl.BlockSpec((1, tq, LANES), lambda bi, hp, i: (bi, i, hp)),
                  pl.BlockSpec((1, s, LANES), lambda bi, hp, i: (bi, 0, hp)),
                  pl.BlockSpec((1, s, LANES), lambda bi, hp, i: (bi, 0, hp)),
                  pl.BlockSpec((1, tq, N_FOX_HEADS), lambda bi, hp, i: (bi, i, 0)),
                  pl.BlockSpec((1, N_FOX_HEADS, s), lambda bi, hp, i: (bi, 0, 0))],
        out_specs=pl.BlockSpec((1, tq, LANES), lambda bi, hp, i: (bi, i, hp)),
        out_shape=jax.ShapeDtypeStruct((b, s, FOX_WIDTH), BF16),
        scratch_shapes=[pltpu.VMEM((2, 2 * tq, tq), F32),
                        pltpu.VMEM((2, 2 * tq, LANES), F32),
                        pltpu.VMEM((2 * tq, LANES), F32),
                        pltpu.VMEM((2 * tq, LANES), F32),
                        pltpu.VMEM((2 * tq, LANES), F32)],
        compiler_params=pltpu.CompilerParams(
            dimension_semantics=("arbitrary", "arbitrary", "arbitrary"),
            vmem_limit_bytes=VMEM_LIMIT),
        name="fox",
    )(fq, fk, fv, c, c_t)


FF_CHUNKS = ((0, 512), (512, 1024), (1024, 1536), (1536, 2048), (2048, 2560), (2560, D_FF))


def _mix_ffn_kernel(x_ref, o0_ref, o1_ref, o2_ref, l0_ref, l1_ref, l2_ref, ob_ref, gate_ref,
                    wd_ref, wfo_ref, wo_ref, g2_ref, wgi_ref, wui_ref, wdn_ref, gfin_ref,
                    out_ref, *tok_refs):
    tm = x_ref.shape[0]
    w = DIL_OUT_WIDTH

    def token_order(ref, dilation, scr_ref):
        if dilation == 1:
            return ref[...].astype(F32)
        for r in range(dilation):
            for cb in range(w // LANES):
                c0 = r * w + cb * LANES
                scr_ref[cb, pl.ds(r, tm // dilation, stride=dilation), :] = (
                    ref[:, c0:c0 + LANES].astype(F32))
        return jnp.concatenate([scr_ref[cb] for cb in range(w // LANES)], axis=1)

    dils = [d for _, d in DIL_PAIRS]
    o0, o1, o2 = (token_order(r, d, s) for r, d, s in
                  zip((o0_ref, o1_ref, o2_ref), dils, (None,) + tok_refs[:2]))
    l0, l1, l2 = (token_order(r, d, s) for r, d, s in
                  zip((l0_ref, l1_ref, l2_ref), dils, (None,) + tok_refs[2:]))
    m = jnp.maximum(jnp.maximum(l0, l1), l2)
    e0, e1, e2 = jnp.exp(l0 - m), jnp.exp(l1 - m), jnp.exp(l2 - m)
    o_a = (e0 * o0 + e1 * o1 + e2 * o2) / (e0 + e1 + e2)
    y_a = _dot(o_a.astype(BF16), wd_ref[...])
    y_b = _dot(ob_ref[...], wfo_ref[...])
    merged = (gate_ref[:, :D_MODEL].astype(F32) * y_a
              + gate_ref[:, D_MODEL:].astype(F32) * y_b)
    x1 = x_ref[...] + _dot(merged.astype(BF16), wo_ref[...])

    h2 = _rms(x1, g2_ref[...]).astype(BF16)
    acc = jnp.zeros_like(x1)
    for c0, c1 in FF_CHUNKS:
        gte = _dot(h2, wgi_ref[:, c0:c1])
        up = _dot(h2, wui_ref[:, c0:c1])
        act = gte * jax.nn.sigmoid(gte) * up
        acc = acc + _dot(act.astype(BF16), wdn_ref[c0:c1, :])
    out_ref[...] = _rms(x1 + acc, gfin_ref[...])


def _mix_ffn(x2d, o_g, lse_g, o_b, gates, w_dil_out, w_fox_out, w_out, norm_ffn_g,
             w_ffn_in, w_ffn_down, norm_final_g, tm):
    n = x2d.shape[0]
    w = DIL_OUT_WIDTH
    dils = [d for _, d in DIL_PAIRS]
    assert all(tm % (16 * d) == 0 for d in dils)

    def const(shape):
        return _resident(shape, lambda i: (0,) * len(shape))

    def tok(width, rows=tm):
        return pl.BlockSpec((rows, width), lambda i: (i, 0))

    dil_specs = [tok(d * w, tm // d) for d in dils]
    in_specs = [tok(D_MODEL)] + dil_specs + dil_specs + [tok(FOX_WIDTH), tok(2 * D_MODEL)]
    in_specs += [const((w, D_MODEL)), const((FOX_WIDTH, D_MODEL)),
                 const((D_MODEL, D_MODEL)), const((1, D_MODEL)),
                 const((D_MODEL, D_FF)), const((D_MODEL, D_FF)), const((D_FF, D_MODEL)),
                 const((1, D_MODEL))]
    return pl.pallas_call(
        _mix_ffn_kernel,
        grid=(n // tm,),
        in_specs=in_specs,
        out_specs=tok(D_MODEL),
        out_shape=jax.ShapeDtypeStruct((n, D_MODEL), F32),
        scratch_shapes=[pltpu.VMEM((w // LANES, tm, LANES), F32)] * 4,
        compiler_params=pltpu.CompilerParams(
            dimension_semantics=("arbitrary",), vmem_limit_bytes=VMEM_LIMIT),
        name="mix_ffn",
    )(x2d, *[o.reshape(n // d, d * w) for o, d in zip(o_g, dils)],
      *[l.reshape(n // d, d * w) for l, d in zip(lse_g, dils)], o_b, gates,
      w_dil_out.astype(BF16), w_fox_out.astype(BF16), w_out.astype(BF16),
      norm_ffn_g.reshape(1, -1),
      w_ffn_in[:, :D_FF].astype(BF16), w_ffn_in[:, D_FF:].astype(BF16),
      w_ffn_down.astype(BF16), norm_final_g.reshape(1, -1))


def kernel(x, norm_mix_g, w_in, b_fgt, b_gate, w_dil_out, w_fox_out, w_out, norm_ffn_g,
           w_ffn_in, w_ffn_down, norm_final_g):
    b, s, d = x.shape
    assert w_in.shape[0] == 1
    layer = 0
    a0, a1, a2, fq, fk, fv, c, gates = _inproj(
        x, norm_mix_g[layer], w_in[layer], b_fgt[layer], b_gate[layer], tm=512)
    o_g, lse_g = zip(*[_dilated(a, g) for g, a in enumerate((a0, a1, a2))])
    o_b = _fox(fq, fk, fv, c, jnp.swapaxes(c, 1, 2), tq=512)
    out = _mix_ffn(x.reshape(b * s, d), o_g, lse_g, o_b.reshape(b * s, FOX_WIDTH),
                   gates.reshape(b * s, 2 * D_MODEL), w_dil_out[layer], w_fox_out[layer],
                   w_out[layer], norm_ffn_g[layer], w_ffn_in[layer], w_ffn_down[layer],
                   norm_final_g, tm=512)
    return out.reshape(b, s, d)
```

```python
import functools

import jax
import jax.numpy as jnp
from jax import lax
from jax.experimental import pallas as pl
from jax.experimental.pallas import tpu as pltpu

D_MODEL = 1024
HEAD_DIM = 64
DIL_PAIRS = ((128, 1), (512, 4), (2048, 16))
N_DIL_GROUPS = len(DIL_PAIRS)
HEADS_PER_DIL_GROUP = 4
N_DIL_HEADS = N_DIL_GROUPS * HEADS_PER_DIL_GROUP
DIL_OUT_WIDTH = HEADS_PER_DIL_GROUP * HEAD_DIM
DIL_QKV_WIDTH = 3 * DIL_OUT_WIDTH
DIL_WIDTH = N_DIL_HEADS * HEAD_DIM
N_FOX_HEADS = 8
FOX_WIDTH = N_FOX_HEADS * HEAD_DIM
D_FF = 2816
RMS_EPS = 1e-6
NEG_INF = -1e30
ATTN_SCALE = HEAD_DIM ** -0.5
LOG2E = 1.4426950408889634

LANES = 128
DIL_BLOCK = 128
VMEM_LIMIT = 56 * 1024 * 1024

F32 = jnp.float32
BF16 = jnp.bfloat16


def _dot(a, b):
    return jnp.dot(a, b, preferred_element_type=F32)


def _dot_nt(a, b):
    return lax.dot_general(a, b, (((1,), (1,)), ((), ())), preferred_element_type=F32)


def _rms(x, g):
    ms = jnp.mean(x * x, axis=-1, keepdims=True)
    return x * lax.rsqrt(ms + RMS_EPS) * g


def _resident(shape, index_map):
    return pl.BlockSpec(shape, index_map, pipeline_mode=pl.Buffered(1))


def _inproj_kernel(x_ref, g_ref, wa0_ref, wa1_ref, wa2_ref, wfq_ref, wfk_ref, wfv_ref,
                   wf_ref, bf_ref, wg_ref, bg_ref,
                   a0_ref, a1_ref, a2_ref, fq_ref, fk_ref, fv_ref, c_ref, gate_ref,
                   carry_ref, y_ref):
    tm = x_ref.shape[1]

    @pl.when(pl.program_id(1) == 0)
    def _():
        carry_ref[...] = jnp.zeros_like(carry_ref)

    h = _rms(x_ref[0], g_ref[...]).astype(BF16)
    for (_, dilation), w_ref, o_ref in zip(DIL_PAIRS, (wa0_ref, wa1_ref, wa2_ref),
                                           (a0_ref, a1_ref, a2_ref)):
        y = _dot(h, w_ref[...])
        if dilation == 1:
            o_ref[0] = y.astype(BF16)
        else:
            for cb in range(DIL_QKV_WIDTH // LANES):
                y_ref[cb] = y[:, cb * LANES:(cb + 1) * LANES]
            for r in range(dilation):
                for cb in range(DIL_QKV_WIDTH // LANES):
                    c0 = r * DIL_QKV_WIDTH + cb * LANES
                    o_ref[0, :, c0:c0 + LANES] = (
                        y_ref[cb, pl.ds(r, tm // dilation, stride=dilation), :].astype(BF16))
    for w_ref, o_ref in ((wfq_ref, fq_ref), (wfk_ref, fk_ref), (wfv_ref, fv_ref)):
        o_ref[0] = _dot(h, w_ref[...]).astype(BF16)
    gate_ref[0] = jax.nn.sigmoid(_dot(h, wg_ref[...]) + bg_ref[...]).astype(BF16)

    z = _dot(h, wf_ref[...]) + bf_ref[...]
    logf = (jnp.minimum(z, 0.0) - jnp.log1p(jnp.exp(-jnp.abs(z)))) * LOG2E
    hi = logf.astype(BF16)
    r1 = logf - hi.astype(F32)
    mid = r1.astype(BF16)
    lo = (r1 - mid.astype(F32)).astype(BF16)
    row = lax.broadcasted_iota(jnp.int32, (tm, tm), 0)
    col = lax.broadcasted_iota(jnp.int32, (tm, tm), 1)
    tril = jnp.where(col <= row, 1.0, 0.0).astype(BF16)
    c = _dot(tril, hi) + _dot(tril, mid) + _dot(tril, lo) + carry_ref[...]
    carry_ref[...] = c[tm - 1:tm, :]
    c_ref[0] = c[:, :N_FOX_HEADS]


def _inproj(x, norm_g, w_in, b_fgt, b_gate, tm):
    b, s, _ = x.shape
    o2 = 3 * DIL_WIDTH
    o3 = o2 + 3 * FOX_WIDTH
    o4 = o3 + N_FOX_HEADS
    wq = w_in[:, :DIL_WIDTH] * ATTN_SCALE
    wk = w_in[:, DIL_WIDTH:2 * DIL_WIDTH]
    wv = w_in[:, 2 * DIL_WIDTH:o2]
    w = DIL_OUT_WIDTH
    wa = [jnp.concatenate([wq[:, g * w:(g + 1) * w], wk[:, g * w:(g + 1) * w],
                           wv[:, g * w:(g + 1) * w]], axis=1).astype(BF16)
          for g in range(N_DIL_GROUPS)]
    wfq = (w_in[:, o2:o2 + FOX_WIDTH] * (ATTN_SCALE * LOG2E)).astype(BF16)
    wfk = w_in[:, o2 + FOX_WIDTH:o2 + 2 * FOX_WIDTH].astype(BF16)
    wfv = w_in[:, o2 + 2 * FOX_WIDTH:o3].astype(BF16)
    wf = jnp.pad(w_in[:, o3:o4], ((0, 0), (0, LANES - N_FOX_HEADS))).astype(BF16)
    bf = jnp.pad(b_fgt, (0, LANES - N_FOX_HEADS)).reshape(1, LANES)
    wg = w_in[:, o4:].astype(BF16)
    bg = b_gate.reshape(1, -1)

    def const(shape):
        return _resident(shape, lambda bi, si: (0,) * len(shape))

    def tok(width, rows=tm):
        return pl.BlockSpec((1, rows, width), lambda bi, si: (bi, si, 0))

    dils = [d for _, d in DIL_PAIRS]
    assert all(tm % (16 * d) == 0 for d in dils)
    out_shape = [jax.ShapeDtypeStruct((b, s // d, d * DIL_QKV_WIDTH), BF16) for d in dils]
    out_shape += [jax.ShapeDtypeStruct((b, s, FOX_WIDTH), BF16)] * 3
    out_shape += [jax.ShapeDtypeStruct((b, s, N_FOX_HEADS), F32),
                  jax.ShapeDtypeStruct((b, s, 2 * D_MODEL), BF16)]
    out_specs = [tok(d * DIL_QKV_WIDTH, tm // d) for d in dils] + [tok(FOX_WIDTH)] * 3
    out_specs += [tok(N_FOX_HEADS), tok(2 * D_MODEL)]
    in_specs = [tok(D_MODEL), const((1, D_MODEL))]
    in_specs += [const((D_MODEL, DIL_QKV_WIDTH))] * 3 + [const((D_MODEL, FOX_WIDTH))] * 3
    in_specs += [const((D_MODEL, LANES)), const((1, LANES)),
                 const((D_MODEL, 2 * D_MODEL)), const((1, 2 * D_MODEL))]
    return pl.pallas_call(
        _inproj_kernel,
        grid=(b, s // tm),
        in_specs=in_specs,
        out_specs=out_specs,
        out_shape=out_shape,
        scratch_shapes=[pltpu.VMEM((1, LANES), F32),
                        pltpu.VMEM((DIL_QKV_WIDTH // LANES, tm, LANES), F32)],
        compiler_params=pltpu.CompilerParams(
            dimension_semantics=("arbitrary", "arbitrary"), vmem_limit_bytes=VMEM_LIMIT),
        name="inproj",
    )(x, norm_g.reshape(1, -1), *wa, wfq, wfk, wfv, wf, bf, wg, bg)


def _dil_kernel(cur_ref, prev_ref, o_ref, lse_ref, *, dilation, slopes, nsub):
    w = DIL_OUT_WIDTH
    blk = DIL_BLOCK
    n = pl.program_id(1)
    row = lax.broadcasted_iota(jnp.int32, (blk, 2 * blk), 0)
    col = lax.broadcasted_iota(jnp.int32, (blk, 2 * blk), 1)
    rel = row - col + blk
    valid = jnp.logical_and(rel >= 0, rel <= blk)
    valid_first = jnp.logical_and(valid, jnp.logical_or(col >= blk, n > 0))
    dist = (rel * dilation).astype(F32)
    bias = [slope * dist for slope in slopes]
    lane = lax.broadcasted_iota(jnp.int32, (blk, LANES), 1)
    low = lane < HEAD_DIM

    for sb in range(nsub):
        r0 = sb * blk
        for hp in range(HEADS_PER_DIL_GROUP // 2):
            lo_, hi_ = hp * LANES, (hp + 1) * LANES
            q_pair = cur_ref[0, r0:r0 + blk, lo_:hi_]
            q2 = jnp.concatenate([jnp.where(low, q_pair, jnp.zeros_like(q_pair)),
                                  jnp.where(low, jnp.zeros_like(q_pair), q_pair)], axis=0)
            if sb == 0:
                k2 = jnp.concatenate([prev_ref[0, :, w + lo_:w + hi_],
                                      cur_ref[0, :blk, w + lo_:w + hi_]], axis=0)
                v2 = jnp.concatenate([prev_ref[0, :, 2 * w + lo_:2 * w + hi_],
                                      cur_ref[0, :blk, 2 * w + lo_:2 * w + hi_]], axis=0)
                ok = valid_first
            else:
                k2 = cur_ref[0, r0 - blk:r0 + blk, w + lo_:w + hi_]
                v2 = cur_ref[0, r0 - blk:r0 + blk, 2 * w + lo_:2 * w + hi_]
                ok = valid
            s2 = _dot_nt(q2, k2)
            ps, dens, lses = [], [], []
            for j in range(2):
                s = jnp.where(ok, s2[j * blk:(j + 1) * blk] - bias[2 * hp + j], NEG_INF)
                m = jnp.max(s, axis=-1, keepdims=True)
                p = jnp.exp(s - m)
                den = jnp.sum(p, axis=-1, keepdims=True)
                ps.append(p.astype(BF16))
                dens.append(den)
                lses.append(jnp.broadcast_to(m + jnp.log(den), (blk, LANES)))
            acc2 = _dot(jnp.concatenate(ps, axis=0), v2)
            o_pair = jnp.where(low, acc2[:blk] / dens[0], acc2[blk:] / dens[1])
            o_ref[0, r0:r0 + blk, lo_:hi_] = o_pair.astype(o_ref.dtype)
            lse_ref[0, r0:r0 + blk, lo_:hi_] = jnp.where(low, lses[0], lses[1])


def _dilated(view, group):
    window, dilation = DIL_PAIRS[group]
    assert window // dilation == DIL_BLOCK
    w = DIL_OUT_WIDTH
    batch, rows, _ = view.shape
    qrows = min(rows, 4 * DIL_BLOCK)
    nsub = qrows // DIL_BLOCK
    slopes = tuple(float(2.0 ** (-8.0 * (group * HEADS_PER_DIL_GROUP + h + 1) / N_DIL_HEADS))
                   for h in range(HEADS_PER_DIL_GROUP))
    return pl.pallas_call(
        functools.partial(_dil_kernel, dilation=dilation, slopes=slopes, nsub=nsub),
        grid=(batch, rows // qrows, dilation),
        in_specs=[pl.BlockSpec((1, qrows, 3 * w), lambda b, n, r: (b, n, r)),
                  pl.BlockSpec((1, DIL_BLOCK, 3 * w),
                               lambda b, n, r: (b, jnp.maximum(n * nsub - 1, 0), r))],
        out_specs=[pl.BlockSpec((1, qrows, w), lambda b, n, r: (b, n, r)),
                   pl.BlockSpec((1, qrows, w), lambda b, n, r: (b, n, r))],
        out_shape=[jax.ShapeDtypeStruct((batch, rows, dilation * w), BF16),
                   jax.ShapeDtypeStruct((batch, rows, dilation * w), F32)],
        compiler_params=pltpu.CompilerParams(
            dimension_semantics=("arbitrary", "arbitrary", "arbitrary")),
        name=f"dilated{group}",
    )(view, view)


def _fox_kernel(q_ref, k_ref, v_ref, cq_ref, ct_ref, o_ref, *scratch, tq, npair):
    hg = pl.program_id(1)
    i = pl.program_id(2)
    lane = lax.broadcasted_iota(jnp.int32, (tq, LANES), 1)
    low = lane < HEAD_DIM
    head_lane = lax.broadcasted_iota(jnp.int32, (tq, N_FOX_HEADS), 1)
    row = lax.broadcasted_iota(jnp.int32, (tq, tq), 0)
    col = lax.broadcasted_iota(jnp.int32, (tq, tq), 1)
    causal = col <= row
    c_blk = cq_ref[0]

    class Stream:
        def __init__(self, pr):
            (self.sa, self.mxa, self.sb, self.mxb,
             self.m, self.v1, self.acc, self.cq) = scratch[8 * pr:8 * (pr + 1)]
            self.lanes = slice(pr * LANES, (pr + 1) * LANES)

            @pl.when(i == 0)
            def _():
                self.v1[:, :LANES] = v_ref[0, :, self.lanes]
                self.v1[:, LANES:] = jnp.ones((self.v1.shape[0], LANES), BF16)

            self.head0 = 2 * (npair * hg + pr)
            q_pair = q_ref[0, :, self.lanes]
            self.q2 = jnp.concatenate([jnp.where(low, q_pair, jnp.zeros_like(q_pair)),
                                       jnp.where(low, jnp.zeros_like(q_pair), q_pair)], axis=0)
            for j in range(2):
                self.cq[j * tq:(j + 1) * tq, :] = jnp.broadcast_to(
                    jnp.sum(jnp.where(head_lane == self.head0 + j, c_blk, 0.0),
                            axis=-1, keepdims=True), (tq, LANES))
            self.m[...] = jnp.full_like(self.m, NEG_INF)
            self.acc[...] = jnp.zeros_like(self.acc)

        def buf(self, which):
            return (self.sa, self.mxa) if which == "a" else (self.sb, self.mxb)

        def produce(self, kt, which, masked):
            s_ref, mx_ref = self.buf(which)
            ks = pl.multiple_of(kt * tq, tq)
            s2 = _dot_nt(self.q2, k_ref[0, pl.ds(ks, tq), self.lanes])
            for j in range(2):
                rows = slice(j * tq, (j + 1) * tq)
                s = s2[rows] - ct_ref[0, pl.ds(self.head0 + j, 1), pl.ds(ks, tq)]
                if masked:
                    s = jnp.where(causal, s, NEG_INF)
                s_ref[rows, :] = s
                mx_ref[rows, :] = jnp.broadcast_to(
                    jnp.max(s, axis=-1, keepdims=True), (tq, LANES)) + self.cq[rows, :]

        def consume(self, kt, which):
            s_ref, mx_ref = self.buf(which)
            ks = pl.multiple_of(kt * tq, tq)
            m = self.m[...]
            m_new = jnp.maximum(m, mx_ref[...])
            alpha = jnp.exp2(m - m_new)
            p = jnp.exp2(s_ref[...] - jnp.tile(m_new - self.cq[...], (1, tq // LANES)))
            self.acc[...] = (jnp.tile(alpha, (1, 2)) * self.acc[...]
                             + _dot(p.astype(BF16), self.v1[pl.ds(ks, tq), :]))
            self.m[...] = m_new

        def finish(self):
            out = self.acc[:, :LANES] / self.acc[:, LANES:]
            o_ref[0, :, self.lanes] = jnp.where(low, out[:tq], out[tq:]).astype(o_ref.dtype)

    streams = [Stream(pr) for pr in range(npair)]

    def tile_in_a(u):
        return jnp.where(u == 0, i, 2 * u - 1)

    def pair(u, carry):
        for st in streams:
            st.consume(tile_in_a(u), "a")
            st.produce(2 * u, "b", False)
        for st in streams:
            st.consume(2 * u, "b")
            st.produce(2 * u + 1, "a", False)
        return carry

    for st in streams:
        st.produce(i, "a", True)
    n_pairs = i // 2
    lax.fori_loop(0, n_pairs, pair, 0)

    @pl.when(i % 2 == 1)
    def _():
        for st in streams:
            st.consume(tile_in_a(n_pairs), "a")
            st.produce(i - 1, "b", False)
        for st in streams:
            st.consume(i - 1, "b")
            st.finish()

    @pl.when(i % 2 == 0)
    def _():
        for st in streams:
            st.consume(tile_in_a(n_pairs), "a")
            st.finish()


def _fox(fq, fk, fv, c, c_t, tq, npair):
    b, s, _ = fq.shape
    rows = 2 * tq
    width = npair * LANES
    return pl.pallas_call(
        functools.partial(_fox_kernel, tq=tq, npair=npair),
        grid=(b, N_FOX_HEADS // (2 * npair), s // tq),
        in_specs=[pl.BlockSpec((1, tq, width), lambda bi, hg, i: (bi, i, hg)),
                  pl.BlockSpec((1, s, width), lambda bi, hg, i: (bi, 0, hg)),
                  pl.BlockSpec((1, s, width), lambda bi, hg, i: (bi, 0, hg)),
                  pl.BlockSpec((1, tq, N_FOX_HEADS), lambda bi, hg, i: (bi, i, 0)),
                  pl.BlockSpec((1, N_FOX_HEADS, s), lambda bi, hg, i: (bi, 0, 0))],
        out_specs=pl.BlockSpec((1, tq, width), lambda bi, hg, i: (bi, i, hg)),
        out_shape=jax.ShapeDtypeStruct((b, s, FOX_WIDTH), BF16),
        scratch_shapes=[pltpu.VMEM((rows, tq), F32),
                        pltpu.VMEM((rows, LANES), F32),
                        pltpu.VMEM((rows, tq), F32),
                        pltpu.VMEM((rows, LANES), F32),
                        pltpu.VMEM((rows, LANES), F32),
                        pltpu.VMEM((s, 2 * LANES), BF16),
                        pltpu.VMEM((rows, 2 * LANES), F32),
                        pltpu.VMEM((rows, LANES), F32)] * npair,
        compiler_params=pltpu.CompilerParams(
            dimension_semantics=("arbitrary", "arbitrary", "arbitrary"),
            vmem_limit_bytes=VMEM_LIMIT),
        name="fox",
    )(fq, fk, fv, c, c_t)


FF_CHUNKS = ((0, 512), (512, 1024), (1024, 1536), (1536, 2048), (2048, 2560), (2560, D_FF))


def _mix_ffn_kernel(x_ref, o0_ref, o1_ref, o2_ref, l0_ref, l1_ref, l2_ref, ob_ref, gate_ref,
                    wd_ref, wfo_ref, wo_ref, g2_ref, wgi_ref, wui_ref, wdn_ref, gfin_ref,
                    out_ref, *tok_refs):
    tm = x_ref.shape[0]
    w = DIL_OUT_WIDTH

    def token_order(ref, dilation, scr_ref):
        if dilation == 1:
            return ref[...].astype(F32)
        for r in range(dilation):
            for cb in range(w // LANES):
                c0 = r * w + cb * LANES
                scr_ref[cb, pl.ds(r, tm // dilation, stride=dilation), :] = (
                    ref[:, c0:c0 + LANES].astype(F32))
        return jnp.concatenate([scr_ref[cb] for cb in range(w // LANES)], axis=1)

    dils = [d for _, d in DIL_PAIRS]
    o0, o1, o2 = (token_order(r, d, s) for r, d, s in
                  zip((o0_ref, o1_ref, o2_ref), dils, (None,) + tok_refs[:2]))
    l0, l1, l2 = (token_order(r, d, s) for r, d, s in
                  zip((l0_ref, l1_ref, l2_ref), dils, (None,) + tok_refs[2:]))
    m = jnp.maximum(jnp.maximum(l0, l1), l2)
    e0, e1, e2 = jnp.exp(l0 - m), jnp.exp(l1 - m), jnp.exp(l2 - m)
    o_a = (e0 * o0 + e1 * o1 + e2 * o2) / (e0 + e1 + e2)
    y_a = _dot(o_a.astype(BF16), wd_ref[...])
    y_b = _dot(ob_ref[...], wfo_ref[...])
    merged = (gate_ref[:, :D_MODEL].astype(F32) * y_a
              + gate_ref[:, D_MODEL:].astype(F32) * y_b)
    x1 = x_ref[...] + _dot(merged.astype(BF16), wo_ref[...])

    h2 = _rms(x1, g2_ref[...]).astype(BF16)
    acc = jnp.zeros_like(x1)
    for c0, c1 in FF_CHUNKS:
        gte = _dot(h2, wgi_ref[:, c0:c1])
        up = _dot(h2, wui_ref[:, c0:c1])
        act = gte * jax.nn.sigmoid(gte) * up
        acc = acc + _dot(act.astype(BF16), wdn_ref[c0:c1, :])
    out_ref[...] = _rms(x1 + acc, gfin_ref[...])


def _mix_ffn(x2d, o_g, lse_g, o_b, gates, w_dil_out, w_fox_out, w_out, norm_ffn_g,
             w_ffn_in, w_ffn_down, norm_final_g, tm):
    n = x2d.shape[0]
    w = DIL_OUT_WIDTH
    dils = [d for _, d in DIL_PAIRS]
    assert all(tm % (16 * d) == 0 for d in dils)

    def const(shape):
        return _resident(shape, lambda i: (0,) * len(shape))

    def tok(width, rows=tm):
        return pl.BlockSpec((rows, width), lambda i: (i, 0))

    dil_specs = [tok(d * w, tm // d) for d in dils]
    in_specs = [tok(D_MODEL)] + dil_specs + dil_specs + [tok(FOX_WIDTH), tok(2 * D_MODEL)]
    in_specs += [const((w, D_MODEL)), const((FOX_WIDTH, D_MODEL)),
                 const((D_MODEL, D_MODEL)), const((1, D_MODEL)),
                 const((D_MODEL, D_FF)), const((D_MODEL, D_FF)), const((D_FF, D_MODEL)),
                 const((1, D_MODEL))]
    return pl.pallas_call(
        _mix_ffn_kernel,
        grid=(n // tm,),
        in_specs=in_specs,
        out_specs=tok(D_MODEL),
        out_shape=jax.ShapeDtypeStruct((n, D_MODEL), F32),
        scratch_shapes=[pltpu.VMEM((w // LANES, tm, LANES), F32)] * 4,
        compiler_params=pltpu.CompilerParams(
            dimension_semantics=("arbitrary",), vmem_limit_bytes=VMEM_LIMIT),
        name="mix_ffn",
    )(x2d, *[o.reshape(n // d, d * w) for o, d in zip(o_g, dils)],
      *[l.reshape(n // d, d * w) for l, d in zip(lse_g, dils)], o_b, gates,
      w_dil_out.astype(BF16), w_fox_out.astype(BF16), w_out.astype(BF16),
      norm_ffn_g.reshape(1, -1),
      w_ffn_in[:, :D_FF].astype(BF16), w_ffn_in[:, D_FF:].astype(BF16),
      w_ffn_down.astype(BF16), norm_final_g.reshape(1, -1))


def kernel(x, norm_mix_g, w_in, b_fgt, b_gate, w_dil_out, w_fox_out, w_out, norm_ffn_g,
           w_ffn_in, w_ffn_down, norm_final_g):
    b, s, d = x.shape
    assert w_in.shape[0] == 1
    layer = 0
    a0, a1, a2, fq, fk, fv, c, gates = _inproj(
        x, norm_mix_g[layer], w_in[layer], b_fgt[layer], b_gate[layer], tm=512)
    o_g, lse_g = zip(*[_dilated(a, g) for g, a in enumerate((a0, a1, a2))])
    o_b = _fox(fq, fk, fv, c, jnp.swapaxes(c, 1, 2), tq=512, npair=1)
    out = _mix_ffn(x.reshape(b * s, d), o_g, lse_g, o_b.reshape(b * s, FOX_WIDTH),
                   gates.reshape(b * s, 2 * D_MODEL), w_dil_out[layer], w_fox_out[layer],
                   w_out[layer], norm_ffn_g[layer], w_ffn_in[layer], w_ffn_down[layer],
                   norm_final_g, tm=512)
    return out.reshape(b, s, d)
```

```python
import functools

import jax
import jax.numpy as jnp
from jax import lax
from jax.experimental import pallas as pl
from jax.experimental.pallas import tpu as pltpu

D_MODEL = 1024
HEAD_DIM = 64
DIL_PAIRS = ((128, 1), (512, 4), (2048, 16))
N_DIL_GROUPS = len(DIL_PAIRS)
HEADS_PER_DIL_GROUP = 4
N_DIL_HEADS = N_DIL_GROUPS * HEADS_PER_DIL_GROUP
DIL_OUT_WIDTH = HEADS_PER_DIL_GROUP * HEAD_DIM
DIL_QKV_WIDTH = 3 * DIL_OUT_WIDTH
DIL_WIDTH = N_DIL_HEADS * HEAD_DIM
N_FOX_HEADS = 8
FOX_WIDTH = N_FOX_HEADS * HEAD_DIM
D_FF = 2816
RMS_EPS = 1e-6
NEG_INF = -1e30
ATTN_SCALE = HEAD_DIM ** -0.5
LOG2E = 1.4426950408889634

LANES = 128
DIL_BLOCK = 128
VMEM_LIMIT = 56 * 1024 * 1024

F32 = jnp.float32
BF16 = jnp.bfloat16


def _dot(a, b):
    return jnp.dot(a, b, preferred_element_type=F32)


def _dot_nt(a, b):
    return lax.dot_general(a, b, (((1,), (1,)), ((), ())), preferred_element_type=F32)


def _rms(x, g):
    ms = jnp.mean(x * x, axis=-1, keepdims=True)
    return x * lax.rsqrt(ms + RMS_EPS) * g


def _resident(shape, index_map):
    return pl.BlockSpec(shape, index_map, pipeline_mode=pl.Buffered(1))


def _inproj_kernel(x_ref, g_ref, wa0_ref, wa1_ref, wa2_ref, wfq_ref, wfk_ref, wfv_ref,
                   wf_ref, bf_ref, wg_ref, bg_ref,
                   a0_ref, a1_ref, a2_ref, fq_ref, fk_ref, fv_ref, c_ref, gate_ref,
                   carry_ref, y_ref):
    tm = x_ref.shape[1]

    @pl.when(pl.program_id(1) == 0)
    def _():
        carry_ref[...] = jnp.zeros_like(carry_ref)

    h = _rms(x_ref[0], g_ref[...]).astype(BF16)
    for (_, dilation), w_ref, o_ref in zip(DIL_PAIRS, (wa0_ref, wa1_ref, wa2_ref),
                                           (a0_ref, a1_ref, a2_ref)):
        y = _dot(h, w_ref[...])
        if dilation == 1:
            o_ref[0] = y.astype(BF16)
        else:
            for cb in range(DIL_QKV_WIDTH // LANES):
                y_ref[cb] = y[:, cb * LANES:(cb + 1) * LANES]
            for r in range(dilation):
                for cb in range(DIL_QKV_WIDTH // LANES):
                    c0 = r * DIL_QKV_WIDTH + cb * LANES
                    o_ref[0, :, c0:c0 + LANES] = (
                        y_ref[cb, pl.ds(r, tm // dilation, stride=dilation), :].astype(BF16))
    for w_ref, o_ref in ((wfq_ref, fq_ref), (wfk_ref, fk_ref), (wfv_ref, fv_ref)):
        o_ref[0] = _dot(h, w_ref[...]).astype(BF16)
    gate_ref[0] = jax.nn.sigmoid(_dot(h, wg_ref[...]) + bg_ref[...]).astype(BF16)

    z = _dot(h, wf_ref[...]) + bf_ref[...]
    logf = (jnp.minimum(z, 0.0) - jnp.log1p(jnp.exp(-jnp.abs(z)))) * LOG2E
    hi = logf.astype(BF16)
    r1 = logf - hi.astype(F32)
    mid = r1.astype(BF16)
    lo = (r1 - mid.astype(F32)).astype(BF16)
    row = lax.broadcasted_iota(jnp.int32, (tm, tm), 0)
    col = lax.broadcasted_iota(jnp.int32, (tm, tm), 1)
    tril = jnp.where(col <= row, 1.0, 0.0).astype(BF16)
    c = _dot(tril, hi) + _dot(tril, mid) + _dot(tril, lo) + carry_ref[...]
    carry_ref[...] = c[tm - 1:tm, :]
    c_ref[0] = c[:, :N_FOX_HEADS]


def _inproj(x, norm_g, w_in, b_fgt, b_gate, tm):
    b, s, _ = x.shape
    o2 = 3 * DIL_WIDTH
    o3 = o2 + 3 * FOX_WIDTH
    o4 = o3 + N_FOX_HEADS
    wq = w_in[:, :DIL_WIDTH] * ATTN_SCALE
    wk = w_in[:, DIL_WIDTH:2 * DIL_WIDTH]
    wv = w_in[:, 2 * DIL_WIDTH:o2]
    w = DIL_OUT_WIDTH
    wa = [jnp.concatenate([wq[:, g * w:(g + 1) * w], wk[:, g * w:(g + 1) * w],
                           wv[:, g * w:(g + 1) * w]], axis=1).astype(BF16)
          for g in range(N_DIL_GROUPS)]
    wfq = (w_in[:, o2:o2 + FOX_WIDTH] * (ATTN_SCALE * LOG2E)).astype(BF16)
    wfk = w_in[:, o2 + FOX_WIDTH:o2 + 2 * FOX_WIDTH].astype(BF16)
    wfv = w_in[:, o2 + 2 * FOX_WIDTH:o3].astype(BF16)
    wf = jnp.pad(w_in[:, o3:o4], ((0, 0), (0, LANES - N_FOX_HEADS))).astype(BF16)
    bf = jnp.pad(b_fgt, (0, LANES - N_FOX_HEADS)).reshape(1, LANES)
    wg = w_in[:, o4:].astype(BF16)
    bg = b_gate.reshape(1, -1)

    def const(shape):
        return _resident(shape, lambda bi, si: (0,) * len(shape))

    def tok(width, rows=tm):
        return pl.BlockSpec((1, rows, width), lambda bi, si: (bi, si, 0))

    dils = [d for _, d in DIL_PAIRS]
    assert all(tm % (16 * d) == 0 for d in dils)
    out_shape = [jax.ShapeDtypeStruct((b, s // d, d * DIL_QKV_WIDTH), BF16) for d in dils]
    out_shape += [jax.ShapeDtypeStruct((b, s, FOX_WIDTH), BF16)] * 3
    out_shape += [jax.ShapeDtypeStruct((b, s, N_FOX_HEADS), F32),
                  jax.ShapeDtypeStruct((b, s, 2 * D_MODEL), BF16)]
    out_specs = [tok(d * DIL_QKV_WIDTH, tm // d) for d in dils] + [tok(FOX_WIDTH)] * 3
    out_specs += [tok(N_FOX_HEADS), tok(2 * D_MODEL)]
    in_specs = [tok(D_MODEL), const((1, D_MODEL))]
    in_specs += [const((D_MODEL, DIL_QKV_WIDTH))] * 3 + [const((D_MODEL, FOX_WIDTH))] * 3
    in_specs += [const((D_MODEL, LANES)), const((1, LANES)),
                 const((D_MODEL, 2 * D_MODEL)), const((1, 2 * D_MODEL))]
    return pl.pallas_call(
        _inproj_kernel,
        grid=(b, s // tm),
        in_specs=in_specs,
        out_specs=out_specs,
        out_shape=out_shape,
        scratch_shapes=[pltpu.VMEM((1, LANES), F32),
                        pltpu.VMEM((DIL_QKV_WIDTH // LANES, tm, LANES), F32)],
        compiler_params=pltpu.CompilerParams(
            dimension_semantics=("arbitrary", "arbitrary"), vmem_limit_bytes=VMEM_LIMIT),
        name="inproj",
    )(x, norm_g.reshape(1, -1), *wa, wfq, wfk, wfv, wf, bf, wg, bg)


def _dil_kernel(cur_ref, prev_ref, o_ref, lse_ref, *, dilation, slopes, nsub):
    w = DIL_OUT_WIDTH
    blk = DIL_BLOCK
    n = pl.program_id(1)
    row = lax.broadcasted_iota(jnp.int32, (blk, 2 * blk), 0)
    col = lax.broadcasted_iota(jnp.int32, (blk, 2 * blk), 1)
    rel = row - col + blk
    valid = jnp.logical_and(rel >= 0, rel <= blk)
    valid_first = jnp.logical_and(valid, jnp.logical_or(col >= blk, n > 0))
    dist = (rel * dilation).astype(F32)
    bias = [slope * dist for slope in slopes]
    lane = lax.broadcasted_iota(jnp.int32, (blk, LANES), 1)
    low = lane < HEAD_DIM

    for sb in range(nsub):
        r0 = sb * blk
        for hp in range(HEADS_PER_DIL_GROUP // 2):
            lo_, hi_ = hp * LANES, (hp + 1) * LANES
            q_pair = cur_ref[0, r0:r0 + blk, lo_:hi_]
            q2 = jnp.concatenate([jnp.where(low, q_pair, jnp.zeros_like(q_pair)),
                                  jnp.where(low, jnp.zeros_like(q_pair), q_pair)], axis=0)
            if sb == 0:
                k2 = jnp.concatenate([prev_ref[0, :, w + lo_:w + hi_],
                                      cur_ref[0, :blk, w + lo_:w + hi_]], axis=0)
                v2 = jnp.concatenate([prev_ref[0, :, 2 * w + lo_:2 * w + hi_],
                                      cur_ref[0, :blk, 2 * w + lo_:2 * w + hi_]], axis=0)
                ok = valid_first
            else:
                k2 = cur_ref[0, r0 - blk:r0 + blk, w + lo_:w + hi_]
                v2 = cur_ref[0, r0 - blk:r0 + blk, 2 * w + lo_:2 * w + hi_]
                ok = valid
            s2 = _dot_nt(q2, k2)
            ps, dens, lses = [], [], []
            for j in range(2):
                s = jnp.where(ok, s2[j * blk:(j + 1) * blk] - bias[2 * hp + j], NEG_INF)
                m = jnp.max(s, axis=-1, keepdims=True)
                p = jnp.exp(s - m)
                den = jnp.sum(p, axis=-1, keepdims=True)
                ps.append(p.astype(BF16))
                dens.append(den)
                lses.append(jnp.broadcast_to(m + jnp.log(den), (blk, LANES)))
            acc2 = _dot(jnp.concatenate(ps, axis=0), v2)
            o_pair = jnp.where(low, acc2[:blk] / dens[0], acc2[blk:] / dens[1])
            o_ref[0, r0:r0 + blk, lo_:hi_] = o_pair.astype(o_ref.dtype)
            lse_ref[0, r0:r0 + blk, lo_:hi_] = jnp.where(low, lses[0], lses[1])


def _dilated(view, group):
    window, dilation = DIL_PAIRS[group]
    assert window // dilation == DIL_BLOCK
    w = DIL_OUT_WIDTH
    batch, rows, _ = view.shape
    qrows = min(rows, 4 * DIL_BLOCK)
    nsub = qrows // DIL_BLOCK
    slopes = tuple(float(2.0 ** (-8.0 * (group * HEADS_PER_DIL_GROUP + h + 1) / N_DIL_HEADS))
                   for h in range(HEADS_PER_DIL_GROUP))
    return pl.pallas_call(
        functools.partial(_dil_kernel, dilation=dilation, slopes=slopes, nsub=nsub),
        grid=(batch, rows // qrows, dilation),
        in_specs=[pl.BlockSpec((1, qrows, 3 * w), lambda b, n, r: (b, n, r)),
                  pl.BlockSpec((1, DIL_BLOCK, 3 * w),
                               lambda b, n, r: (b, jnp.maximum(n * nsub - 1, 0), r))],
        out_specs=[pl.BlockSpec((1, qrows, w), lambda b, n, r: (b, n, r)),
                   pl.BlockSpec((1, qrows, w), lambda b, n, r: (b, n, r))],
        out_shape=[jax.ShapeDtypeStruct((batch, rows, dilation * w), BF16),
                   jax.ShapeDtypeStruct((batch, rows, dilation * w), F32)],
        compiler_params=pltpu.CompilerParams(
            dimension_semantics=("arbitrary", "arbitrary", "arbitrary")),
        name=f"dilated{group}",
    )(view, view)


def _fox_kernel(q_ref, k_ref, v_ref, c_ref, ct_ref, o_ref,
                sa_ref, mxa_ref, sb_ref, mxb_ref, sc_ref, mxc_ref,
                m0_ref, acc0_ref, cq0_ref, m1_ref, acc1_ref, cq1_ref, v1_ref, *, tq):
    hp = pl.program_id(1)
    j = pl.program_id(2)
    n_qtiles = k_ref.shape[1] // tq
    lane = lax.broadcasted_iota(jnp.int32, (tq, LANES), 1)
    low = lane < HEAD_DIM
    head_lane = lax.broadcasted_iota(jnp.int32, (tq, N_FOX_HEADS), 1)
    row = lax.broadcasted_iota(jnp.int32, (tq, tq), 0)
    col = lax.broadcasted_iota(jnp.int32, (tq, tq), 1)
    causal = col <= row

    @pl.when(j == 0)
    def _():
        v1_ref[:, :LANES] = v_ref[0]
        v1_ref[:, LANES:] = jnp.ones((v1_ref.shape[0], LANES), BF16)

    bufs = ((sa_ref, mxa_ref), (sb_ref, mxb_ref), (sc_ref, mxc_ref))
    states = ((m0_ref, acc0_ref, cq0_ref), (m1_ref, acc1_ref, cq1_ref))

    class QTile:
        def __init__(self, qi, state):
            self.qi = qi
            self.m, self.acc, self.cq = state
            self.rows = slice(qi * tq, (qi + 1) * tq)
            q_pair = q_ref[0, self.rows, :]
            self.q2 = jnp.concatenate([jnp.where(low, q_pair, jnp.zeros_like(q_pair)),
                                       jnp.where(low, jnp.zeros_like(q_pair), q_pair)], axis=0)
            c_blk = c_ref[0, self.rows, :]
            for h in range(2):
                self.cq[h * tq:(h + 1) * tq, :] = jnp.broadcast_to(
                    jnp.sum(jnp.where(head_lane == 2 * hp + h, c_blk, 0.0),
                            axis=-1, keepdims=True), (tq, LANES))
            self.m[...] = jnp.full_like(self.m, NEG_INF)
            self.acc[...] = jnp.zeros_like(self.acc)

        def produce(self, kt, buf):
            s_ref, mx_ref = buf
            keys = slice(kt * tq, (kt + 1) * tq)
            s2 = _dot_nt(self.q2, k_ref[0, keys, :])
            for h in range(2):
                rows = slice(h * tq, (h + 1) * tq)
                s = s2[rows] - ct_ref[0, pl.ds(2 * hp + h, 1), keys]
                if kt == self.qi:
                    s = jnp.where(causal, s, NEG_INF)
                s_ref[rows, :] = s
                mx_ref[rows, :] = jnp.broadcast_to(
                    jnp.max(s, axis=-1, keepdims=True), (tq, LANES)) + self.cq[rows, :]

        def consume(self, kt, buf):
            s_ref, mx_ref = buf
            m = self.m[...]
            m_new = jnp.maximum(m, mx_ref[...])
            alpha = jnp.exp2(m - m_new)
            p = jnp.exp2(s_ref[...] - jnp.tile(m_new - self.cq[...], (1, tq // LANES)))
            self.acc[...] = (jnp.tile(alpha, (1, 2)) * self.acc[...]
                             + _dot(p.astype(BF16), v1_ref[kt * tq:(kt + 1) * tq, :]))
            self.m[...] = m_new

        def finish(self):
            out = self.acc[:, :LANES] / self.acc[:, LANES:]
            o_ref[0, self.rows, :] = jnp.where(low, out[:tq], out[tq:]).astype(o_ref.dtype)

    def schedule(step):
        work = []
        for qi, state in zip((step, n_qtiles - 1 - step), states):
            qt = QTile(qi, state)
            work += [(qt, kt) for kt in [qi] + list(range(qi))]
        ahead = len(bufs) - 1
        for n in range(min(ahead, len(work))):
            qt, kt = work[n]
            qt.produce(kt, bufs[n])
        for n, (qt, kt) in enumerate(work):
            qt.consume(kt, bufs[n % len(bufs)])
            if n + 1 == len(work) or work[n + 1][0] is not qt:
                qt.finish()
            if n + ahead < len(work):
                qt2, kt2 = work[n + ahead]
                qt2.produce(kt2, bufs[(n + ahead) % len(bufs)])

    assert n_qtiles % 2 == 0
    for step in range(n_qtiles // 2):
        pl.when(j == step)(functools.partial(schedule, step))


def _fox(fq, fk, fv, c, c_t, tq):
    b, s, _ = fq.shape
    rows = 2 * tq
    n_qtiles = s // tq

    def whole(width):
        return pl.BlockSpec((1, s, width), lambda bi, hp, j: (bi, 0, hp))

    return pl.pallas_call(
        functools.partial(_fox_kernel, tq=tq),
        grid=(b, N_FOX_HEADS // 2, n_qtiles // 2),
        in_specs=[whole(LANES), whole(LANES), whole(LANES),
                  pl.BlockSpec((1, s, N_FOX_HEADS), lambda bi, hp, j: (bi, 0, 0)),
                  pl.BlockSpec((1, N_FOX_HEADS, s), lambda bi, hp, j: (bi, 0, 0))],
        out_specs=whole(LANES),
        out_shape=jax.ShapeDtypeStruct((b, s, FOX_WIDTH), BF16),
        scratch_shapes=[pltpu.VMEM((rows, tq), F32),
                        pltpu.VMEM((rows, LANES), F32)] * 3
        + [pltpu.VMEM((rows, LANES), F32),
           pltpu.VMEM((rows, 2 * LANES), F32),
           pltpu.VMEM((rows, LANES), F32)] * 2
        + [pltpu.VMEM((s, 2 * LANES), BF16)],
        compiler_params=pltpu.CompilerParams(
            dimension_semantics=("arbitrary", "arbitrary", "arbitrary"),
            vmem_limit_bytes=VMEM_LIMIT),
        name="fox",
    )(fq, fk, fv, c, c_t)


FF_CHUNKS = ((0, 512), (512, 1024), (1024, 1536), (1536, 2048), (2048, 2560), (2560, D_FF))


def _mix_ffn_kernel(x_ref, o0_ref, o1_ref, o2_ref, l0_ref, l1_ref, l2_ref, ob_ref, gate_ref,
                    wd_ref, wfo_ref, wo_ref, g2_ref, wgi_ref, wui_ref, wdn_ref, gfin_ref,
                    out_ref, *tok_refs):
    tm = x_ref.shape[0]
    w = DIL_OUT_WIDTH

    def token_order(ref, dilation, scr_ref):
        if dilation == 1:
            return ref[...].astype(F32)
        for r in range(dilation):
            for cb in range(w // LANES):
                c0 = r * w + cb * LANES
                scr_ref[cb, pl.ds(r, tm // dilation, stride=dilation), :] = (
                    ref[:, c0:c0 + LANES].astype(F32))
        return jnp.concatenate([scr_ref[cb] for cb in range(w // LANES)], axis=1)

    dils = [d for _, d in DIL_PAIRS]
    o0, o1, o2 = (token_order(r, d, s) for r, d, s in
                  zip((o0_ref, o1_ref, o2_ref), dils, (None,) + tok_refs[:2]))
    l0, l1, l2 = (token_order(r, d, s) for r, d, s in
                  zip((l0_ref, l1_ref, l2_ref), dils, (None,) + tok_refs[2:]))
    m = jnp.maximum(jnp.maximum(l0, l1), l2)
    e0, e1, e2 = jnp.exp(l0 - m), jnp.exp(l1 - m), jnp.exp(l2 - m)
    o_a = (e0 * o0 + e1 * o1 + e2 * o2) / (e0 + e1 + e2)
    y_a = _dot(o_a.astype(BF16), wd_ref[...])
    y_b = _dot(ob_ref[...], wfo_ref[...])
    merged = (gate_ref[:, :D_MODEL].astype(F32) * y_a
              + gate_ref[:, D_MODEL:].astype(F32) * y_b)
    x1 = x_ref[...] + _dot(merged.astype(BF16), wo_ref[...])

    h2 = _rms(x1, g2_ref[...]).astype(BF16)
    acc = jnp.zeros_like(x1)
    for c0, c1 in FF_CHUNKS:
        gte = _dot(h2, wgi_ref[:, c0:c1])
        up = _dot(h2, wui_ref[:, c0:c1])
        act = gte * jax.nn.sigmoid(gte) * up
        acc = acc + _dot(act.astype(BF16), wdn_ref[c0:c1, :])
    out_ref[...] = _rms(x1 + acc, gfin_ref[...])


def _mix_ffn(x2d, o_g, lse_g, o_b, gates, w_dil_out, w_fox_out, w_out, norm_ffn_g,
             w_ffn_in, w_ffn_down, norm_final_g, tm):
    n = x2d.shape[0]
    w = DIL_OUT_WIDTH
    dils = [d for _, d in DIL_PAIRS]
    assert all(tm % (16 * d) == 0 for d in dils)

    def const(shape):
        return _resident(shape, lambda i: (0,) * len(shape))

    def tok(width, rows=tm):
        return pl.BlockSpec((rows, width), lambda i: (i, 0))

    dil_specs = [tok(d * w, tm // d) for d in dils]
    in_specs = [tok(D_MODEL)] + dil_specs + dil_specs + [tok(FOX_WIDTH), tok(2 * D_MODEL)]
    in_specs += [const((w, D_MODEL)), const((FOX_WIDTH, D_MODEL)),
                 const((D_MODEL, D_MODEL)), const((1, D_MODEL)),
                 const((D_MODEL, D_FF)), const((D_MODEL, D_FF)), const((D_FF, D_MODEL)),
                 const((1, D_MODEL))]
    return pl.pallas_call(
        _mix_ffn_kernel,
        grid=(n // tm,),
        in_specs=in_specs,
        out_specs=tok(D_MODEL),
        out_shape=jax.ShapeDtypeStruct((n, D_MODEL), F32),
        scratch_shapes=[pltpu.VMEM((w // LANES, tm, LANES), F32)] * 4,
        compiler_params=pltpu.CompilerParams(
            dimension_semantics=("arbitrary",), vmem_limit_bytes=VMEM_LIMIT),
        name="mix_ffn",
    )(x2d, *[o.reshape(n // d, d * w) for o, d in zip(o_g, dils)],
      *[l.reshape(n // d, d * w) for l, d in zip(lse_g, dils)], o_b, gates,
      w_dil_out.astype(BF16), w_fox_out.astype(BF16), w_out.astype(BF16),
      norm_ffn_g.reshape(1, -1),
      w_ffn_in[:, :D_FF].astype(BF16), w_ffn_in[:, D_FF:].astype(BF16),
      w_ffn_down.astype(BF16), norm_final_g.reshape(1, -1))


def kernel(x, norm_mix_g, w_in, b_fgt, b_gate, w_dil_out, w_fox_out, w_out, norm_ffn_g,
           w_ffn_in, w_ffn_down, norm_final_g):
    b, s, d = x.shape
    assert w_in.shape[0] == 1
    layer = 0
    a0, a1, a2, fq, fk, fv, c, gates = _inproj(
        x, norm_mix_g[layer], w_in[layer], b_fgt[layer], b_gate[layer], tm=512)
    o_g, lse_g = zip(*[_dilated(a, g) for g, a in enumerate((a0, a1, a2))])
    o_b = _fox(fq, fk, fv, c, jnp.swapaxes(c, 1, 2), tq=512)
    out = _mix_ffn(x.reshape(b * s, d), o_g, lse_g, o_b.reshape(b * s, FOX_WIDTH),
                   gates.reshape(b * s, 2 * D_MODEL), w_dil_out[layer], w_fox_out[layer],
                   w_out[layer], norm_ffn_g[layer], w_ffn_in[layer], w_ffn_down[layer],
                   norm_final_g, tm=512)
    return out.reshape(b, s, d)
```

```python
import functools

import jax
import jax.numpy as jnp
from jax import lax
from jax.experimental import pallas as pl
from jax.experimental.pallas import tpu as pltpu

D_MODEL = 1024
HEAD_DIM = 64
DIL_PAIRS = ((128, 1), (512, 4), (2048, 16))
N_DIL_GROUPS = len(DIL_PAIRS)
HEADS_PER_DIL_GROUP = 4
N_DIL_HEADS = N_DIL_GROUPS * HEADS_PER_DIL_GROUP
DIL_OUT_WIDTH = HEADS_PER_DIL_GROUP * HEAD_DIM
DIL_QKV_WIDTH = 3 * DIL_OUT_WIDTH
DIL_WIDTH = N_DIL_HEADS * HEAD_DIM
N_FOX_HEADS = 8
FOX_WIDTH = N_FOX_HEADS * HEAD_DIM
D_FF = 2816
RMS_EPS = 1e-6
NEG_INF = -1e30
ATTN_SCALE = HEAD_DIM ** -0.5
LOG2E = 1.4426950408889634

LANES = 128
DIL_BLOCK = 128
F_ROWS = 16
VMEM_LIMIT = 56 * 1024 * 1024

F32 = jnp.float32
BF16 = jnp.bfloat16


def _dot(a, b):
    return jnp.dot(a, b, preferred_element_type=F32)


def _dot_nt(a, b):
    return lax.dot_general(a, b, (((1,), (1,)), ((), ())), preferred_element_type=F32)


def _rms(x, g):
    ms = jnp.mean(x * x, axis=-1, keepdims=True)
    return x * lax.rsqrt(ms + RMS_EPS) * g


def _resident(shape, index_map):
    return pl.BlockSpec(shape, index_map, pipeline_mode=pl.Buffered(1))


def _inproj_kernel(x_ref, g_ref, wa0_ref, wa1_ref, wa2_ref, wfq_ref, wfk_ref, wfv_ref,
                   wf_ref, bf_ref, wg_ref, bg_ref,
                   a0_ref, a1_ref, a2_ref, fq_ref, fk_ref, fv_ref, c_ref, ct_ref, gate_ref,
                   carry_ref, y_ref):
    tm = x_ref.shape[1]

    @pl.when(pl.program_id(1) == 0)
    def _():
        carry_ref[...] = jnp.zeros_like(carry_ref)

    h = _rms(x_ref[0], g_ref[...]).astype(BF16)
    for (_, dilation), w_ref, o_ref in zip(DIL_PAIRS, (wa0_ref, wa1_ref, wa2_ref),
                                           (a0_ref, a1_ref, a2_ref)):
        y = _dot(h, w_ref[...])
        if dilation == 1:
            o_ref[0] = y.astype(BF16)
        else:
            for cb in range(DIL_QKV_WIDTH // LANES):
                y_ref[cb] = y[:, cb * LANES:(cb + 1) * LANES]
            for r in range(dilation):
                for cb in range(DIL_QKV_WIDTH // LANES):
                    c0 = r * DIL_QKV_WIDTH + cb * LANES
                    o_ref[0, :, c0:c0 + LANES] = (
                        y_ref[cb, pl.ds(r, tm // dilation, stride=dilation), :].astype(BF16))
    for w_ref, o_ref in ((wfq_ref, fq_ref), (wfk_ref, fk_ref), (wfv_ref, fv_ref)):
        o_ref[0] = _dot(h, w_ref[...]).astype(BF16)
    gate_ref[0] = jax.nn.sigmoid(_dot(h, wg_ref[...]) + bg_ref[...]).astype(BF16)

    z = _dot_nt(wf_ref[...], h) + bf_ref[...]
    logf = (jnp.minimum(z, 0.0) - jnp.log1p(jnp.exp(-jnp.abs(z)))) * LOG2E
    hi = logf.astype(BF16)
    r1 = logf - hi.astype(F32)
    mid = r1.astype(BF16)
    lo = (r1 - mid.astype(F32)).astype(BF16)
    row = lax.broadcasted_iota(jnp.int32, (tm, tm), 0)
    col = lax.broadcasted_iota(jnp.int32, (tm, tm), 1)
    upto = jnp.where(row <= col, 1.0, 0.0).astype(BF16)
    c3 = _dot(jnp.concatenate([hi, mid, lo], axis=0), upto)
    c_t = c3[:F_ROWS] + c3[F_ROWS:2 * F_ROWS] + c3[2 * F_ROWS:] + carry_ref[...]
    carry_ref[...] = c_t[:, tm - 1:tm]
    ct_ref[0] = c_t[:N_FOX_HEADS]
    c_pad = jnp.concatenate([c_t, jnp.zeros((LANES - F_ROWS, tm), F32)], axis=0)
    c_ref[0] = c_pad.T[:, :N_FOX_HEADS]


def _inproj(x, norm_g, w_in, b_fgt, b_gate, tm):
    b, s, _ = x.shape
    o2 = 3 * DIL_WIDTH
    o3 = o2 + 3 * FOX_WIDTH
    o4 = o3 + N_FOX_HEADS
    wq = w_in[:, :DIL_WIDTH] * (ATTN_SCALE * LOG2E)
    wk = w_in[:, DIL_WIDTH:2 * DIL_WIDTH]
    wv = w_in[:, 2 * DIL_WIDTH:o2]
    w = DIL_OUT_WIDTH
    wa = [jnp.concatenate([wq[:, g * w:(g + 1) * w], wk[:, g * w:(g + 1) * w],
                           wv[:, g * w:(g + 1) * w]], axis=1).astype(BF16)
          for g in range(N_DIL_GROUPS)]
    wfq = (w_in[:, o2:o2 + FOX_WIDTH] * (ATTN_SCALE * LOG2E)).astype(BF16)
    wfk = w_in[:, o2 + FOX_WIDTH:o2 + 2 * FOX_WIDTH].astype(BF16)
    wfv = w_in[:, o2 + 2 * FOX_WIDTH:o3].astype(BF16)
    wf = jnp.pad(w_in[:, o3:o4].T, ((0, F_ROWS - N_FOX_HEADS), (0, 0))).astype(BF16)
    bf = jnp.pad(b_fgt, (0, F_ROWS - N_FOX_HEADS)).reshape(F_ROWS, 1)
    wg = w_in[:, o4:].astype(BF16)
    bg = b_gate.reshape(1, -1)

    def const(shape):
        return _resident(shape, lambda bi, si: (0,) * len(shape))

    def tok(width, rows=tm):
        return pl.BlockSpec((1, rows, width), lambda bi, si: (bi, si, 0))

    dils = [d for _, d in DIL_PAIRS]
    assert all(tm % (16 * d) == 0 for d in dils)
    out_shape = [jax.ShapeDtypeStruct((b, s // d, d * DIL_QKV_WIDTH), BF16) for d in dils]
    out_shape += [jax.ShapeDtypeStruct((b, s, FOX_WIDTH), BF16)] * 3
    out_shape += [jax.ShapeDtypeStruct((b, s, N_FOX_HEADS), F32),
                  jax.ShapeDtypeStruct((b, N_FOX_HEADS, s), F32),
                  jax.ShapeDtypeStruct((b, s, 2 * D_MODEL), BF16)]
    out_specs = [tok(d * DIL_QKV_WIDTH, tm // d) for d in dils] + [tok(FOX_WIDTH)] * 3
    out_specs += [tok(N_FOX_HEADS),
                  pl.BlockSpec((1, N_FOX_HEADS, tm), lambda bi, si: (bi, 0, si)),
                  tok(2 * D_MODEL)]
    in_specs = [tok(D_MODEL), const((1, D_MODEL))]
    in_specs += [const((D_MODEL, DIL_QKV_WIDTH))] * 3 + [const((D_MODEL, FOX_WIDTH))] * 3
    in_specs += [const((F_ROWS, D_MODEL)), const((F_ROWS, 1)),
                 const((D_MODEL, 2 * D_MODEL)), const((1, 2 * D_MODEL))]
    return pl.pallas_call(
        _inproj_kernel,
        grid=(b, s // tm),
        in_specs=in_specs,
        out_specs=out_specs,
        out_shape=out_shape,
        scratch_shapes=[pltpu.VMEM((F_ROWS, 1), F32),
                        pltpu.VMEM((DIL_QKV_WIDTH // LANES, tm, LANES), F32)],
        compiler_params=pltpu.CompilerParams(
            dimension_semantics=("arbitrary", "arbitrary"), vmem_limit_bytes=VMEM_LIMIT),
        name="inproj",
    )(x, norm_g.reshape(1, -1), *wa, wfq, wfk, wfv, wf, bf, wg, bg)


def _dil_kernel(cur_ref, prev_ref, o_ref, lse_ref, *, dilation, slopes, nsub, nres):
    w = DIL_OUT_WIDTH
    blk = DIL_BLOCK
    n = pl.program_id(1)
    row = lax.broadcasted_iota(jnp.int32, (blk, 2 * blk), 0)
    col = lax.broadcasted_iota(jnp.int32, (blk, 2 * blk), 1)
    rel = row - col + blk
    valid = jnp.logical_and(rel >= 0, rel <= blk)
    valid_first = jnp.logical_and(valid, jnp.logical_or(col >= blk, n > 0))
    dist = (rel * dilation).astype(F32)
    penalty = [[jnp.where(ok, (slope * LOG2E) * dist, -NEG_INF) for slope in slopes]
               for ok in (valid_first, valid)]
    lane = lax.broadcasted_iota(jnp.int32, (blk, LANES), 1)
    low = lane < HEAD_DIM

    for res in range(nres):
        c0 = res * 3 * w
        for sb in range(nsub):
            r0 = sb * blk
            for hp in range(HEADS_PER_DIL_GROUP // 2):
                lo_, hi_ = c0 + hp * LANES, c0 + (hp + 1) * LANES
                q_pair = cur_ref[0, r0:r0 + blk, lo_:hi_]
                q2 = jnp.concatenate([jnp.where(low, q_pair, jnp.zeros_like(q_pair)),
                                      jnp.where(low, jnp.zeros_like(q_pair), q_pair)], axis=0)
                if sb == 0:
                    k2 = jnp.concatenate([prev_ref[0, :, w + lo_:w + hi_],
                                          cur_ref[0, :blk, w + lo_:w + hi_]], axis=0)
                    v2 = jnp.concatenate([prev_ref[0, :, 2 * w + lo_:2 * w + hi_],
                                          cur_ref[0, :blk, 2 * w + lo_:2 * w + hi_]], axis=0)
                else:
                    k2 = cur_ref[0, r0 - blk:r0 + blk, w + lo_:w + hi_]
                    v2 = cur_ref[0, r0 - blk:r0 + blk, 2 * w + lo_:2 * w + hi_]
                s2 = _dot_nt(q2, k2)
                ps, ms = [], []
                for j in range(2):
                    s = s2[j * blk:(j + 1) * blk] - penalty[min(sb, 1)][2 * hp + j]
                    m = jnp.max(s, axis=-1, keepdims=True)
                    ps.append(jnp.exp2(s - m).astype(BF16))
                    ms.append(jnp.broadcast_to(m, (blk, LANES)))
                acc2 = _dot(jnp.concatenate(ps, axis=0),
                            jnp.concatenate([v2, jnp.ones_like(v2)], axis=1))
                den = acc2[:, LANES:]
                out = acc2[:, :LANES] / den
                lse = jnp.concatenate(ms, axis=0) + jnp.log2(den)
                oc = res * w + hp * LANES
                o_ref[0, r0:r0 + blk, oc:oc + LANES] = jnp.where(
                    low, out[:blk], out[blk:]).astype(o_ref.dtype)
                lse_ref[0, r0:r0 + blk, oc:oc + LANES] = jnp.where(low, lse[:blk], lse[blk:])


def _dilated(view, group):
    window, dilation = DIL_PAIRS[group]
    assert window // dilation == DIL_BLOCK
    w = DIL_OUT_WIDTH
    batch, rows, _ = view.shape
    qrows = min(rows, 4 * DIL_BLOCK)
    nsub = qrows // DIL_BLOCK
    nres = min(dilation, 4 // nsub)
    slopes = tuple(float(2.0 ** (-8.0 * (group * HEADS_PER_DIL_GROUP + h + 1) / N_DIL_HEADS))
                   for h in range(HEADS_PER_DIL_GROUP))
    return pl.pallas_call(
        functools.partial(_dil_kernel, dilation=dilation, slopes=slopes, nsub=nsub, nres=nres),
        grid=(batch, rows // qrows, dilation // nres),
        in_specs=[pl.BlockSpec((1, qrows, nres * 3 * w), lambda b, n, r: (b, n, r)),
                  pl.BlockSpec((1, DIL_BLOCK, nres * 3 * w),
                               lambda b, n, r: (b, jnp.maximum(n * nsub - 1, 0), r))],
        out_specs=[pl.BlockSpec((1, qrows, nres * w), lambda b, n, r: (b, n, r)),
                   pl.BlockSpec((1, qrows, nres * w), lambda b, n, r: (b, n, r))],
        out_shape=[jax.ShapeDtypeStruct((batch, rows, dilation * w), BF16),
                   jax.ShapeDtypeStruct((batch, rows, dilation * w), F32)],
        compiler_params=pltpu.CompilerParams(
            dimension_semantics=("arbitrary", "arbitrary", "arbitrary")),
        name=f"dilated{group}",
    )(view, view)


def _fox_kernel(q_ref, k_ref, v_ref, c_ref, ct_ref, o_ref,
                sa_ref, mxa_ref, sb_ref, mxb_ref, sc_ref, mxc_ref,
                m0_ref, acc0_ref, cq0_ref, m1_ref, acc1_ref, cq1_ref, v1_ref, *, tq):
    hp = pl.program_id(1)
    j = pl.program_id(2)
    n_qtiles = k_ref.shape[1] // tq
    lane = lax.broadcasted_iota(jnp.int32, (tq, LANES), 1)
    low = lane < HEAD_DIM
    head_lane = lax.broadcasted_iota(jnp.int32, (tq, N_FOX_HEADS), 1)
    row = lax.broadcasted_iota(jnp.int32, (tq, tq), 0)
    col = lax.broadcasted_iota(jnp.int32, (tq, tq), 1)
    causal = col <= row

    @pl.when(j == 0)
    def _():
        v1_ref[:, :LANES] = v_ref[0]
        v1_ref[:, LANES:] = jnp.ones((v1_ref.shape[0], LANES), BF16)

    bufs = ((sa_ref, mxa_ref), (sb_ref, mxb_ref), (sc_ref, mxc_ref))
    states = ((m0_ref, acc0_ref, cq0_ref), (m1_ref, acc1_ref, cq1_ref))

    class QTile:
        def __init__(self, qi, state):
            self.qi = qi
            self.m, self.acc, self.cq = state
            self.rows = slice(qi * tq, (qi + 1) * tq)
            q_pair = q_ref[0, self.rows, :]
            self.q2 = jnp.concatenate([jnp.where(low, q_pair, jnp.zeros_like(q_pair)),
                                       jnp.where(low, jnp.zeros_like(q_pair), q_pair)], axis=0)
            c_blk = c_ref[0, self.rows, :]
            for h in range(2):
                self.cq[h * tq:(h + 1) * tq, :] = jnp.broadcast_to(
                    jnp.sum(jnp.where(head_lane == 2 * hp + h, c_blk, 0.0),
                            axis=-1, keepdims=True), (tq, LANES))
            self.m[...] = jnp.full_like(self.m, NEG_INF)
            self.acc[...] = jnp.zeros_like(self.acc)

        def produce(self, kt, buf):
            s_ref, mx_ref = buf
            keys = slice(kt * tq, (kt + 1) * tq)
            s2 = _dot_nt(self.q2, k_ref[0, keys, :])
            for h in range(2):
                rows = slice(h * tq, (h + 1) * tq)
                s = s2[rows] - ct_ref[0, pl.ds(2 * hp + h, 1), keys]
                if kt == self.qi:
                    s = jnp.where(causal, s, NEG_INF)
                s_ref[rows, :] = s
                mx_ref[rows, :] = jnp.broadcast_to(
                    jnp.max(s, axis=-1, keepdims=True), (tq, LANES)) + self.cq[rows, :]

        def consume(self, kt, buf):
            s_ref, mx_ref = buf
            m = self.m[...]
            m_new = jnp.maximum(m, mx_ref[...])
            alpha = jnp.exp2(m - m_new)
            p = jnp.exp2(s_ref[...] - jnp.tile(m_new - self.cq[...], (1, tq // LANES)))
            self.acc[...] = (jnp.tile(alpha, (1, 2)) * self.acc[...]
                             + _dot(p.astype(BF16), v1_ref[kt * tq:(kt + 1) * tq, :]))
            self.m[...] = m_new

        def finish(self):
            out = self.acc[:, :LANES] / self.acc[:, LANES:]
            o_ref[0, self.rows, :] = jnp.where(low, out[:tq], out[tq:]).astype(o_ref.dtype)

    def schedule(step):
        work = []
        for qi, state in zip((step, n_qtiles - 1 - step), states):
            qt = QTile(qi, state)
            work += [(qt, kt) for kt in [qi] + list(range(qi))]
        ahead = len(bufs) - 1
        for n in range(min(ahead, len(work))):
            qt, kt = work[n]
            qt.produce(kt, bufs[n])
        for n, (qt, kt) in enumerate(work):
            qt.consume(kt, bufs[n % len(bufs)])
            if n + 1 == len(work) or work[n + 1][0] is not qt:
                qt.finish()
            if n + ahead < len(work):
                qt2, kt2 = work[n + ahead]
                qt2.produce(kt2, bufs[(n + ahead) % len(bufs)])

    assert n_qtiles % 2 == 0
    for step in range(n_qtiles // 2):
        pl.when(j == step)(functools.partial(schedule, step))


def _fox(fq, fk, fv, c, c_t, tq):
    b, s, _ = fq.shape
    rows = 2 * tq
    n_qtiles = s // tq

    def whole(width):
        return pl.BlockSpec((1, s, width), lambda bi, hp, j: (bi, 0, hp))

    return pl.pallas_call(
        functools.partial(_fox_kernel, tq=tq),
        grid=(b, N_FOX_HEADS // 2, n_qtiles // 2),
        in_specs=[whole(LANES), whole(LANES), whole(LANES),
                  pl.BlockSpec((1, s, N_FOX_HEADS), lambda bi, hp, j: (bi, 0, 0)),
                  pl.BlockSpec((1, N_FOX_HEADS, s), lambda bi, hp, j: (bi, 0, 0))],
        out_specs=whole(LANES),
        out_shape=jax.ShapeDtypeStruct((b, s, FOX_WIDTH), BF16),
        scratch_shapes=[pltpu.VMEM((rows, tq), F32),
                        pltpu.VMEM((rows, LANES), F32)] * 3
        + [pltpu.VMEM((rows, LANES), F32),
           pltpu.VMEM((rows, 2 * LANES), F32),
           pltpu.VMEM((rows, LANES), F32)] * 2
        + [pltpu.VMEM((s, 2 * LANES), BF16)],
        compiler_params=pltpu.CompilerParams(
            dimension_semantics=("arbitrary", "arbitrary", "arbitrary"),
            vmem_limit_bytes=VMEM_LIMIT),
        name="fox",
    )(fq, fk, fv, c, c_t)


FF_CHUNKS = ((0, 512), (512, 1024), (1024, 1536), (1536, 2048), (2048, 2560), (2560, D_FF))


def _mix_ffn_kernel(x_ref, o0_ref, o1_ref, o2_ref, l0_ref, l1_ref, l2_ref, ob_ref, gate_ref,
                    wd_ref, wfo_ref, wo_ref, g2_ref, wgi_ref, wui_ref, wdn_ref, gfin_ref,
                    out_ref, *tok_refs):
    tm = x_ref.shape[0]
    w = DIL_OUT_WIDTH

    def token_order(ref, dilation, scr_ref):
        if dilation == 1:
            return ref[...].astype(F32)
        for r in range(dilation):
            for cb in range(w // LANES):
                c0 = r * w + cb * LANES
                scr_ref[cb, pl.ds(r, tm // dilation, stride=dilation), :] = (
                    ref[:, c0:c0 + LANES].astype(F32))
        return jnp.concatenate([scr_ref[cb] for cb in range(w // LANES)], axis=1)

    dils = [d for _, d in DIL_PAIRS]
    o0, o1, o2 = (token_order(r, d, s) for r, d, s in
                  zip((o0_ref, o1_ref, o2_ref), dils, (None,) + tok_refs[:2]))
    l0, l1, l2 = (token_order(r, d, s) for r, d, s in
                  zip((l0_ref, l1_ref, l2_ref), dils, (None,) + tok_refs[2:]))
    m = jnp.maximum(jnp.maximum(l0, l1), l2)
    e0, e1, e2 = jnp.exp2(l0 - m), jnp.exp2(l1 - m), jnp.exp2(l2 - m)
    o_a = (e0 * o0 + e1 * o1 + e2 * o2) / (e0 + e1 + e2)
    y_a = _dot(o_a.astype(BF16), wd_ref[...])
    y_b = _dot(ob_ref[...], wfo_ref[...])
    merged = (gate_ref[:, :D_MODEL].astype(F32) * y_a
              + gate_ref[:, D_MODEL:].astype(F32) * y_b)
    x1 = x_ref[...] + _dot(merged.astype(BF16), wo_ref[...])

    h2 = _rms(x1, g2_ref[...]).astype(BF16)
    acc = jnp.zeros_like(x1)
    for c0, c1 in FF_CHUNKS:
        gte = _dot(h2, wgi_ref[:, c0:c1])
        up = _dot(h2, wui_ref[:, c0:c1])
        act = gte * jax.nn.sigmoid(gte) * up
        acc = acc + _dot(act.astype(BF16), wdn_ref[c0:c1, :])
    out_ref[...] = _rms(x1 + acc, gfin_ref[...])


def _mix_ffn(x2d, o_g, lse_g, o_b, gates, w_dil_out, w_fox_out, w_out, norm_ffn_g,
             w_ffn_in, w_ffn_down, norm_final_g, tm):
    n = x2d.shape[0]
    w = DIL_OUT_WIDTH
    dils = [d for _, d in DIL_PAIRS]
    assert all(tm % (16 * d) == 0 for d in dils)

    def const(shape):
        return _resident(shape, lambda i: (0,) * len(shape))

    def tok(width, rows=tm):
        return pl.BlockSpec((rows, width), lambda i: (i, 0))

    dil_specs = [tok(d * w, tm // d) for d in dils]
    in_specs = [tok(D_MODEL)] + dil_specs + dil_specs + [tok(FOX_WIDTH), tok(2 * D_MODEL)]
    in_specs += [const((w, D_MODEL)), const((FOX_WIDTH, D_MODEL)),
                 const((D_MODEL, D_MODEL)), const((1, D_MODEL)),
                 const((D_MODEL, D_FF)), const((D_MODEL, D_FF)), const((D_FF, D_MODEL)),
                 const((1, D_MODEL))]
    return pl.pallas_call(
        _mix_ffn_kernel,
        grid=(n // tm,),
        in_specs=in_specs,
        out_specs=tok(D_MODEL),
        out_shape=jax.ShapeDtypeStruct((n, D_MODEL), F32),
        scratch_shapes=[pltpu.VMEM((w // LANES, tm, LANES), F32)] * 4,
        compiler_params=pltpu.CompilerParams(
            dimension_semantics=("arbitrary",), vmem_limit_bytes=VMEM_LIMIT),
        name="mix_ffn",
    )(x2d, *[o.reshape(n // d, d * w) for o, d in zip(o_g, dils)],
      *[l.reshape(n // d, d * w) for l, d in zip(lse_g, dils)], o_b, gates,
      w_dil_out.astype(BF16), w_fox_out.astype(BF16), w_out.astype(BF16),
      norm_ffn_g.reshape(1, -1),
      w_ffn_in[:, :D_FF].astype(BF16), w_ffn_in[:, D_FF:].astype(BF16),
      w_ffn_down.astype(BF16), norm_final_g.reshape(1, -1))


def kernel(x, norm_mix_g, w_in, b_fgt, b_gate, w_dil_out, w_fox_out, w_out, norm_ffn_g,
           w_ffn_in, w_ffn_down, norm_final_g):
    b, s, d = x.shape
    assert w_in.shape[0] == 1
    layer = 0
    a0, a1, a2, fq, fk, fv, c, c_t, gates = _inproj(
        x, norm_mix_g[layer], w_in[layer], b_fgt[layer], b_gate[layer], tm=512)
    o_g, lse_g = zip(*[_dilated(a, g) for g, a in enumerate((a0, a1, a2))])
    o_b = _fox(fq, fk, fv, c, c_t, tq=512)
    out = _mix_ffn(x.reshape(b * s, d), o_g, lse_g, o_b.reshape(b * s, FOX_WIDTH),
                   gates.reshape(b * s, 2 * D_MODEL), w_dil_out[layer], w_fox_out[layer],
                   w_out[layer], norm_ffn_g[layer], w_ffn_in[layer], w_ffn_down[layer],
                   norm_final_g, tm=512)
    return out.reshape(b, s, d)
```

```python
import functools

import jax
import jax.numpy as jnp
from jax import lax
from jax.experimental import pallas as pl
from jax.experimental.pallas import tpu as pltpu

D_MODEL = 1024
HEAD_DIM = 64
DIL_PAIRS = ((128, 1), (512, 4), (2048, 16))
N_DIL_GROUPS = len(DIL_PAIRS)
HEADS_PER_DIL_GROUP = 4
N_DIL_HEADS = N_DIL_GROUPS * HEADS_PER_DIL_GROUP
DIL_OUT_WIDTH = HEADS_PER_DIL_GROUP * HEAD_DIM
DIL_QKV_WIDTH = 3 * DIL_OUT_WIDTH
DIL_WIDTH = N_DIL_HEADS * HEAD_DIM
N_FOX_HEADS = 8
FOX_WIDTH = N_FOX_HEADS * HEAD_DIM
D_FF = 2816
RMS_EPS = 1e-6
NEG_INF = -1e30
ATTN_SCALE = HEAD_DIM ** -0.5
LOG2E = 1.4426950408889634

LANES = 128
DIL_BLOCK = 128
F_ROWS = 16
VMEM_LIMIT = 56 * 1024 * 1024

F32 = jnp.float32
BF16 = jnp.bfloat16


def _dot(a, b):
    return jnp.dot(a, b, preferred_element_type=F32)


def _dot_nt(a, b):
    return lax.dot_general(a, b, (((1,), (1,)), ((), ())), preferred_element_type=F32)


def _rms_scale(x):
    return lax.rsqrt(jnp.mean(x * x, axis=-1, keepdims=True) + RMS_EPS)


def _rms(x, g):
    return x * _rms_scale(x) * g


def _resident(shape, index_map):
    return pl.BlockSpec(shape, index_map, pipeline_mode=pl.Buffered(1))


def _inproj_kernel(x_ref, g_ref, wa0_ref, wa1_ref, wa2_ref, wfq_ref, wfk_ref, wfv_ref,
                   wf_ref, bf_ref, wg_ref, bg_ref,
                   a0_ref, a1_ref, a2_ref, fq_ref, fk_ref, fv_ref, c_ref, ct_ref, gate_ref,
                   carry_ref, *y_refs):
    tm = x_ref.shape[1]

    @pl.when(pl.program_id(1) == 0)
    def _():
        carry_ref[...] = jnp.zeros_like(carry_ref)

    x = x_ref[0]
    h = (x * g_ref[...]).astype(BF16)
    scale = _rms_scale(x)
    scale_t = jnp.broadcast_to(scale, (tm, LANES)).T[:F_ROWS]

    z = _dot_nt(wf_ref[...], h) * scale_t + bf_ref[...]
    logf = (jnp.minimum(z, 0.0) - jnp.log1p(jnp.exp(-jnp.abs(z)))) * LOG2E
    hi = logf.astype(BF16)
    r1 = logf - hi.astype(F32)
    mid = r1.astype(BF16)
    lo = (r1 - mid.astype(F32)).astype(BF16)
    row = lax.broadcasted_iota(jnp.int32, (tm, tm), 0)
    col = lax.broadcasted_iota(jnp.int32, (tm, tm), 1)
    upto = jnp.where(row <= col, 1.0, 0.0).astype(BF16)
    c3 = _dot(jnp.concatenate([hi, mid, lo], axis=0), upto)
    c_t = c3[:F_ROWS] + c3[F_ROWS:2 * F_ROWS] + c3[2 * F_ROWS:] + carry_ref[...]
    carry_ref[...] = c_t[:, tm - 1:tm]
    ct_ref[0] = c_t[:N_FOX_HEADS]
    c_pad = jnp.concatenate([c_t, jnp.zeros((LANES - F_ROWS, tm), F32)], axis=0)
    c_ref[0] = c_pad.T[:, :N_FOX_HEADS]

    for (_, dilation), w_ref, o_ref, y_ref in zip(DIL_PAIRS, (wa0_ref, wa1_ref, wa2_ref),
                                                  (a0_ref, a1_ref, a2_ref), (None,) + y_refs):
        y = _dot(h, w_ref[...]) * scale
        if dilation == 1:
            o_ref[0] = y.astype(BF16)
        else:
            for cb in range(DIL_QKV_WIDTH // LANES):
                y_ref[cb] = y[:, cb * LANES:(cb + 1) * LANES]
            for r in range(dilation):
                for cb in range(DIL_QKV_WIDTH // LANES):
                    c0 = r * DIL_QKV_WIDTH + cb * LANES
                    o_ref[0, :, c0:c0 + LANES] = (
                        y_ref[cb, pl.ds(r, tm // dilation, stride=dilation), :].astype(BF16))
    for w_ref, o_ref in ((wfq_ref, fq_ref), (wfk_ref, fk_ref), (wfv_ref, fv_ref)):
        o_ref[0] = (_dot(h, w_ref[...]) * scale).astype(BF16)
    gate_ref[0] = jax.nn.sigmoid(_dot(h, wg_ref[...]) * scale + bg_ref[...]).astype(BF16)


def _inproj(x, norm_g, w_in, b_fgt, b_gate, tm):
    b, s, _ = x.shape
    o2 = 3 * DIL_WIDTH
    o3 = o2 + 3 * FOX_WIDTH
    o4 = o3 + N_FOX_HEADS
    wq = w_in[:, :DIL_WIDTH] * (ATTN_SCALE * LOG2E)
    wk = w_in[:, DIL_WIDTH:2 * DIL_WIDTH]
    wv = w_in[:, 2 * DIL_WIDTH:o2]
    w = DIL_OUT_WIDTH
    wa = [jnp.concatenate([wq[:, g * w:(g + 1) * w], wk[:, g * w:(g + 1) * w],
                           wv[:, g * w:(g + 1) * w]], axis=1).astype(BF16)
          for g in range(N_DIL_GROUPS)]
    wfq = (w_in[:, o2:o2 + FOX_WIDTH] * (ATTN_SCALE * LOG2E)).astype(BF16)
    wfk = w_in[:, o2 + FOX_WIDTH:o2 + 2 * FOX_WIDTH].astype(BF16)
    wfv = w_in[:, o2 + 2 * FOX_WIDTH:o3].astype(BF16)
    wf = jnp.pad(w_in[:, o3:o4].T, ((0, F_ROWS - N_FOX_HEADS), (0, 0))).astype(BF16)
    bf = jnp.pad(b_fgt, (0, F_ROWS - N_FOX_HEADS)).reshape(F_ROWS, 1)
    wg = w_in[:, o4:].astype(BF16)
    bg = b_gate.reshape(1, -1)

    def const(shape):
        return _resident(shape, lambda bi, si: (0,) * len(shape))

    def tok(width, rows=tm):
        return pl.BlockSpec((1, rows, width), lambda bi, si: (bi, si, 0))

    dils = [d for _, d in DIL_PAIRS]
    assert all(tm % (16 * d) == 0 for d in dils)
    out_shape = [jax.ShapeDtypeStruct((b, s // d, d * DIL_QKV_WIDTH), BF16) for d in dils]
    out_shape += [jax.ShapeDtypeStruct((b, s, FOX_WIDTH), BF16)] * 3
    out_shape += [jax.ShapeDtypeStruct((b, s, N_FOX_HEADS), F32),
                  jax.ShapeDtypeStruct((b, N_FOX_HEADS, s), F32),
                  jax.ShapeDtypeStruct((b, s, 2 * D_MODEL), BF16)]
    out_specs = [tok(d * DIL_QKV_WIDTH, tm // d) for d in dils] + [tok(FOX_WIDTH)] * 3
    out_specs += [tok(N_FOX_HEADS),
                  pl.BlockSpec((1, N_FOX_HEADS, tm), lambda bi, si: (bi, 0, si)),
                  tok(2 * D_MODEL)]
    in_specs = [tok(D_MODEL), const((1, D_MODEL))]
    in_specs += [const((D_MODEL, DIL_QKV_WIDTH))] * 3 + [const((D_MODEL, FOX_WIDTH))] * 3
    in_specs += [const((F_ROWS, D_MODEL)), const((F_ROWS, 1)),
                 const((D_MODEL, 2 * D_MODEL)), const((1, 2 * D_MODEL))]
    return pl.pallas_call(
        _inproj_kernel,
        grid=(b, s // tm),
        in_specs=in_specs,
        out_specs=out_specs,
        out_shape=out_shape,
        scratch_shapes=[pltpu.VMEM((F_ROWS, 1), F32),
                        pltpu.VMEM((DIL_QKV_WIDTH // LANES, tm, LANES), F32),
                        pltpu.VMEM((DIL_QKV_WIDTH // LANES, tm, LANES), F32)],
        compiler_params=pltpu.CompilerParams(
            dimension_semantics=("arbitrary", "arbitrary"), vmem_limit_bytes=VMEM_LIMIT),
        name="inproj",
    )(x, norm_g.reshape(1, -1), *wa, wfq, wfk, wfv, wf, bf, wg, bg)


def _dil_kernel(cur_ref, prev_ref, o_ref, lse_ref, *, dilation, slopes, nsub, nres):
    w = DIL_OUT_WIDTH
    blk = DIL_BLOCK
    n = pl.program_id(1)
    row = lax.broadcasted_iota(jnp.int32, (blk, 2 * blk), 0)
    col = lax.broadcasted_iota(jnp.int32, (blk, 2 * blk), 1)
    rel = row - col + blk
    valid = jnp.logical_and(rel >= 0, rel <= blk)
    valid_first = jnp.logical_and(valid, jnp.logical_or(col >= blk, n > 0))
    dist = (rel * dilation).astype(F32)
    penalty = [[jnp.where(ok, (slope * LOG2E) * dist, -NEG_INF) for slope in slopes]
               for ok in (valid_first, valid)]
    lane = lax.broadcasted_iota(jnp.int32, (blk, LANES), 1)
    low = lane < HEAD_DIM

    for res in range(nres):
        c0 = res * 3 * w
        for sb in range(nsub):
            r0 = sb * blk
            for hp in range(HEADS_PER_DIL_GROUP // 2):
                lo_, hi_ = c0 + hp * LANES, c0 + (hp + 1) * LANES
                q_pair = cur_ref[0, r0:r0 + blk, lo_:hi_]
                q2 = jnp.concatenate([jnp.where(low, q_pair, jnp.zeros_like(q_pair)),
                                      jnp.where(low, jnp.zeros_like(q_pair), q_pair)], axis=0)
                if sb == 0:
                    k2 = jnp.concatenate([prev_ref[0, :, w + lo_:w + hi_],
                                          cur_ref[0, :blk, w + lo_:w + hi_]], axis=0)
                    v2 = jnp.concatenate([prev_ref[0, :, 2 * w + lo_:2 * w + hi_],
                                          cur_ref[0, :blk, 2 * w + lo_:2 * w + hi_]], axis=0)
                else:
                    k2 = cur_ref[0, r0 - blk:r0 + blk, w + lo_:w + hi_]
                    v2 = cur_ref[0, r0 - blk:r0 + blk, 2 * w + lo_:2 * w + hi_]
                s2 = _dot_nt(q2, k2)
                ps, ms = [], []
                for j in range(2):
                    s = s2[j * blk:(j + 1) * blk] - penalty[min(sb, 1)][2 * hp + j]
                    m = jnp.max(s, axis=-1, keepdims=True)
                    ps.append(jnp.exp2(s - m).astype(BF16))
                    ms.append(jnp.broadcast_to(m, (blk, LANES)))
                acc2 = _dot(jnp.concatenate(ps, axis=0),
                            jnp.concatenate([v2, jnp.ones_like(v2)], axis=1))
                den = acc2[:, LANES:]
                out = acc2[:, :LANES] / den
                lse = jnp.concatenate(ms, axis=0) + jnp.log2(den)
                oc = res * w + hp * LANES
                o_ref[0, r0:r0 + blk, oc:oc + LANES] = jnp.where(
                    low, out[:blk], out[blk:]).astype(o_ref.dtype)
                lse_ref[0, r0:r0 + blk, oc:oc + LANES] = jnp.where(low, lse[:blk], lse[blk:])


def _dilated(view, group):
    window, dilation = DIL_PAIRS[group]
    assert window // dilation == DIL_BLOCK
    w = DIL_OUT_WIDTH
    batch, rows, _ = view.shape
    qrows = min(rows, 4 * DIL_BLOCK)
    nsub = qrows // DIL_BLOCK
    nres = min(dilation, 4 // nsub)
    slopes = tuple(float(2.0 ** (-8.0 * (group * HEADS_PER_DIL_GROUP + h + 1) / N_DIL_HEADS))
                   for h in range(HEADS_PER_DIL_GROUP))
    return pl.pallas_call(
        functools.partial(_dil_kernel, dilation=dilation, slopes=slopes, nsub=nsub, nres=nres),
        grid=(batch, rows // qrows, dilation // nres),
        in_specs=[pl.BlockSpec((1, qrows, nres * 3 * w), lambda b, n, r: (b, n, r)),
                  pl.BlockSpec((1, DIL_BLOCK, nres * 3 * w),
                               lambda b, n, r: (b, jnp.maximum(n * nsub - 1, 0), r))],
        out_specs=[pl.BlockSpec((1, qrows, nres * w), lambda b, n, r: (b, n, r)),
                   pl.BlockSpec((1, qrows, nres * w), lambda b, n, r: (b, n, r))],
        out_shape=[jax.ShapeDtypeStruct((batch, rows, dilation * w), BF16),
                   jax.ShapeDtypeStruct((batch, rows, dilation * w), F32)],
        compiler_params=pltpu.CompilerParams(
            dimension_semantics=("arbitrary", "arbitrary", "arbitrary")),
        name=f"dilated{group}",
    )(view, view)


def _fox_kernel(q_ref, k_ref, v_ref, c_ref, ct_ref, o_ref,
                sa_ref, mxa_ref, sb_ref, mxb_ref, sc_ref, mxc_ref,
                m0_ref, acc0_ref, cq0_ref, m1_ref, acc1_ref, cq1_ref, v1_ref, *, tq):
    hp = pl.program_id(1)
    j = pl.program_id(2)
    n_qtiles = k_ref.shape[1] // tq
    lane = lax.broadcasted_iota(jnp.int32, (tq, LANES), 1)
    low = lane < HEAD_DIM
    head_lane = lax.broadcasted_iota(jnp.int32, (tq, N_FOX_HEADS), 1)
    row = lax.broadcasted_iota(jnp.int32, (tq, tq), 0)
    col = lax.broadcasted_iota(jnp.int32, (tq, tq), 1)
    causal = col <= row

    @pl.when(j == 0)
    def _():
        v1_ref[:, :LANES] = v_ref[0]
        v1_ref[:, LANES:] = jnp.ones((v1_ref.shape[0], LANES), BF16)

    bufs = ((sa_ref, mxa_ref), (sb_ref, mxb_ref), (sc_ref, mxc_ref))
    states = ((m0_ref, acc0_ref, cq0_ref), (m1_ref, acc1_ref, cq1_ref))

    class QTile:
        def __init__(self, qi, state):
            self.qi = qi
            self.m, self.acc, self.cq = state
            self.rows = slice(qi * tq, (qi + 1) * tq)
            q_pair = q_ref[0, self.rows, :]
            self.q2 = jnp.concatenate([jnp.where(low, q_pair, jnp.zeros_like(q_pair)),
                                       jnp.where(low, jnp.zeros_like(q_pair), q_pair)], axis=0)
            c_blk = c_ref[0, self.rows, :]
            for h in range(2):
                self.cq[h * tq:(h + 1) * tq, :] = jnp.broadcast_to(
                    jnp.sum(jnp.where(head_lane == 2 * hp + h, c_blk, 0.0),
                            axis=-1, keepdims=True), (tq, LANES))
            self.m[...] = jnp.full_like(self.m, NEG_INF)
            self.acc[...] = jnp.zeros_like(self.acc)

        def produce(self, kt, buf):
            s_ref, mx_ref = buf
            keys = slice(kt * tq, (kt + 1) * tq)
            s2 = _dot_nt(self.q2, k_ref[0, keys, :])
            for h in range(2):
                rows = slice(h * tq, (h + 1) * tq)
                s = s2[rows] - ct_ref[0, pl.ds(2 * hp + h, 1), keys]
                if kt == self.qi:
                    s = jnp.where(causal, s, NEG_INF)
                s_ref[rows, :] = s
                mx_ref[rows, :] = jnp.broadcast_to(
                    jnp.max(s, axis=-1, keepdims=True), (tq, LANES)) + self.cq[rows, :]

        def consume(self, kt, buf):
            s_ref, mx_ref = buf
            m = self.m[...]
            m_new = jnp.maximum(m, mx_ref[...])
            alpha = jnp.exp2(m - m_new)
            p = jnp.exp2(s_ref[...] - jnp.tile(m_new - self.cq[...], (1, tq // LANES)))
            self.acc[...] = (jnp.tile(alpha, (1, 2)) * self.acc[...]
                             + _dot(p.astype(BF16), v1_ref[kt * tq:(kt + 1) * tq, :]))
            self.m[...] = m_new

        def finish(self):
            out = self.acc[:, :LANES] / self.acc[:, LANES:]
            o_ref[0, self.rows, :] = jnp.where(low, out[:tq], out[tq:]).astype(o_ref.dtype)

    def schedule(step):
        work = []
        for qi, state in zip((step, n_qtiles - 1 - step), states):
            qt = QTile(qi, state)
            work += [(qt, kt) for kt in [qi] + list(range(qi))]
        ahead = len(bufs) - 1
        for n in range(min(ahead, len(work))):
            qt, kt = work[n]
            qt.produce(kt, bufs[n])
        for n, (qt, kt) in enumerate(work):
            qt.consume(kt, bufs[n % len(bufs)])
            if n + 1 == len(work) or work[n + 1][0] is not qt:
                qt.finish()
            if n + ahead < len(work):
                qt2, kt2 = work[n + ahead]
                qt2.produce(kt2, bufs[(n + ahead) % len(bufs)])

    assert n_qtiles % 2 == 0
    for step in range(n_qtiles // 2):
        pl.when(j == step)(functools.partial(schedule, step))


def _fox(fq, fk, fv, c, c_t, tq):
    b, s, _ = fq.shape
    rows = 2 * tq
    n_qtiles = s // tq

    def whole(width):
        return pl.BlockSpec((1, s, width), lambda bi, hp, j: (bi, 0, hp))

    return pl.pallas_call(
        functools.partial(_fox_kernel, tq=tq),
        grid=(b, N_FOX_HEADS // 2, n_qtiles // 2),
        in_specs=[whole(LANES), whole(LANES), whole(LANES),
                  pl.BlockSpec((1, s, N_FOX_HEADS), lambda bi, hp, j: (bi, 0, 0)),
                  pl.BlockSpec((1, N_FOX_HEADS, s), lambda bi, hp, j: (bi, 0, 0))],
        out_specs=whole(LANES),
        out_shape=jax.ShapeDtypeStruct((b, s, FOX_WIDTH), BF16),
        scratch_shapes=[pltpu.VMEM((rows, tq), F32),
                        pltpu.VMEM((rows, LANES), F32)] * 3
        + [pltpu.VMEM((rows, LANES), F32),
           pltpu.VMEM((rows, 2 * LANES), F32),
           pltpu.VMEM((rows, LANES), F32)] * 2
        + [pltpu.VMEM((s, 2 * LANES), BF16)],
        compiler_params=pltpu.CompilerParams(
            dimension_semantics=("arbitrary", "arbitrary", "arbitrary"),
            vmem_limit_bytes=VMEM_LIMIT),
        name="fox",
    )(fq, fk, fv, c, c_t)


FF_CHUNKS = ((0, 512), (512, 1024), (1024, 1536), (1536, 2048), (2048, 2560), (2560, D_FF))


def _mix_ffn_kernel(x_ref, o0_ref, o1_ref, o2_ref, l0_ref, l1_ref, l2_ref, ob_ref, gate_ref,
                    wd_ref, wfo_ref, wo_ref, g2_ref, wgi_ref, wui_ref, wdn_ref, gfin_ref,
                    out_ref, *tok_refs):
    tm = x_ref.shape[0]
    w = DIL_OUT_WIDTH

    def token_order(ref, dilation, scr_ref):
        if dilation == 1:
            return ref[...].astype(F32)
        for r in range(dilation):
            for cb in range(w // LANES):
                c0 = r * w + cb * LANES
                scr_ref[cb, pl.ds(r, tm // dilation, stride=dilation), :] = (
                    ref[:, c0:c0 + LANES].astype(F32))
        return jnp.concatenate([scr_ref[cb] for cb in range(w // LANES)], axis=1)

    dils = [d for _, d in DIL_PAIRS]
    o0, o1, o2 = (token_order(r, d, s) for r, d, s in
                  zip((o0_ref, o1_ref, o2_ref), dils, (None,) + tok_refs[:2]))
    l0, l1, l2 = (token_order(r, d, s) for r, d, s in
                  zip((l0_ref, l1_ref, l2_ref), dils, (None,) + tok_refs[2:]))
    m = jnp.maximum(jnp.maximum(l0, l1), l2)
    e0, e1, e2 = jnp.exp2(l0 - m), jnp.exp2(l1 - m), jnp.exp2(l2 - m)
    o_a = (e0 * o0 + e1 * o1 + e2 * o2) / (e0 + e1 + e2)
    y_a = _dot(o_a.astype(BF16), wd_ref[...])
    y_b = _dot(ob_ref[...], wfo_ref[...])
    merged = (gate_ref[:, :D_MODEL].astype(F32) * y_a
              + gate_ref[:, D_MODEL:].astype(F32) * y_b)
    x1 = x_ref[...] + _dot(merged.astype(BF16), wo_ref[...])

    h2 = (x1 * g2_ref[...]).astype(BF16)
    scale = _rms_scale(x1)
    acc = jnp.zeros_like(x1)
    for c0, c1 in FF_CHUNKS:
        gte = _dot(h2, wgi_ref[:, c0:c1]) * scale
        up = _dot(h2, wui_ref[:, c0:c1]) * scale
        act = gte * jax.nn.sigmoid(gte) * up
        acc = acc + _dot(act.astype(BF16), wdn_ref[c0:c1, :])
    out_ref[...] = _rms(x1 + acc, gfin_ref[...])


def _mix_ffn(x2d, o_g, lse_g, o_b, gates, w_dil_out, w_fox_out, w_out, norm_ffn_g,
             w_ffn_in, w_ffn_down, norm_final_g, tm):
    n = x2d.shape[0]
    w = DIL_OUT_WIDTH
    dils = [d for _, d in DIL_PAIRS]
    assert all(tm % (16 * d) == 0 for d in dils)

    def const(shape):
        return _resident(shape, lambda i: (0,) * len(shape))

    def tok(width, rows=tm):
        return pl.BlockSpec((rows, width), lambda i: (i, 0))

    dil_specs = [tok(d * w, tm // d) for d in dils]
    in_specs = [tok(D_MODEL)] + dil_specs + dil_specs + [tok(FOX_WIDTH), tok(2 * D_MODEL)]
    in_specs += [const((w, D_MODEL)), const((FOX_WIDTH, D_MODEL)),
                 const((D_MODEL, D_MODEL)), const((1, D_MODEL)),
                 const((D_MODEL, D_FF)), const((D_MODEL, D_FF)), const((D_FF, D_MODEL)),
                 const((1, D_MODEL))]
    return pl.pallas_call(
        _mix_ffn_kernel,
        grid=(n // tm,),
        in_specs=in_specs,
        out_specs=tok(D_MODEL),
        out_shape=jax.ShapeDtypeStruct((n, D_MODEL), F32),
        scratch_shapes=[pltpu.VMEM((w // LANES, tm, LANES), F32)] * 4,
        compiler_params=pltpu.CompilerParams(
            dimension_semantics=("arbitrary",), vmem_limit_bytes=VMEM_LIMIT),
        name="mix_ffn",
    )(x2d, *[o.reshape(n // d, d * w) for o, d in zip(o_g, dils)],
      *[l.reshape(n // d, d * w) for l, d in zip(lse_g, dils)], o_b, gates,
      w_dil_out.astype(BF16), w_fox_out.astype(BF16), w_out.astype(BF16),
      norm_ffn_g.reshape(1, -1),
      w_ffn_in[:, :D_FF].astype(BF16), w_ffn_in[:, D_FF:].astype(BF16),
      w_ffn_down.astype(BF16), norm_final_g.reshape(1, -1))


def kernel(x, norm_mix_g, w_in, b_fgt, b_gate, w_dil_out, w_fox_out, w_out, norm_ffn_g,
           w_ffn_in, w_ffn_down, norm_final_g):
    b, s, d = x.shape
    assert w_in.shape[0] == 1
    layer = 0
    a0, a1, a2, fq, fk, fv, c, c_t, gates = _inproj(
        x, norm_mix_g[layer], w_in[layer], b_fgt[layer], b_gate[layer], tm=512)
    o_g, lse_g = zip(*[_dilated(a, g) for g, a in enumerate((a0, a1, a2))])
    o_b = _fox(fq, fk, fv, c, c_t, tq=512)
    out = _mix_ffn(x.reshape(b * s, d), o_g, lse_g, o_b.reshape(b * s, FOX_WIDTH),
                   gates.reshape(b * s, 2 * D_MODEL), w_dil_out[layer], w_fox_out[layer],
                   w_out[layer], norm_ffn_g[layer], w_ffn_in[layer], w_ffn_down[layer],
                   norm_final_g, tm=512)
    return out.reshape(b, s, d)
```

```python
import functools

import jax
import jax.numpy as jnp
from jax import lax
from jax.experimental import pallas as pl
from jax.experimental.pallas import tpu as pltpu

D_MODEL = 1024
HEAD_DIM = 64
DIL_PAIRS = ((128, 1), (512, 4), (2048, 16))
N_DIL_GROUPS = len(DIL_PAIRS)
HEADS_PER_DIL_GROUP = 4
N_DIL_HEADS = N_DIL_GROUPS * HEADS_PER_DIL_GROUP
DIL_OUT_WIDTH = HEADS_PER_DIL_GROUP * HEAD_DIM
DIL_QKV_WIDTH = 3 * DIL_OUT_WIDTH
DIL_WIDTH = N_DIL_HEADS * HEAD_DIM
N_FOX_HEADS = 8
FOX_WIDTH = N_FOX_HEADS * HEAD_DIM
D_FF = 2816
RMS_EPS = 1e-6
NEG_INF = -1e30
ATTN_SCALE = HEAD_DIM ** -0.5
LOG2E = 1.4426950408889634

LANES = 128
DIL_BLOCK = 128
F_ROWS = 16
VMEM_LIMIT = 56 * 1024 * 1024

F32 = jnp.float32
BF16 = jnp.bfloat16


def _dot(a, b):
    return jnp.dot(a, b, preferred_element_type=F32)


def _dot_nt(a, b):
    return lax.dot_general(a, b, (((1,), (1,)), ((), ())), preferred_element_type=F32)


def _rms_scale(x):
    return lax.rsqrt(jnp.mean(x * x, axis=-1, keepdims=True) + RMS_EPS)


def _rms(x, g):
    return x * _rms_scale(x) * g


def _resident(shape, index_map):
    return pl.BlockSpec(shape, index_map, pipeline_mode=pl.Buffered(1))


def _inproj_kernel(x_ref, g_ref, wa0_ref, wa1_ref, wa2_ref, wfq_ref, wfk_ref, wfv_ref,
                   wf_ref, bf_ref, wg_ref, bg_ref,
                   a0_ref, a1_ref, a2_ref, fq_ref, fk_ref, fv_ref, c_ref, ct_ref, gate_ref,
                   carry_ref, *y_refs):
    tm = x_ref.shape[1]

    @pl.when(pl.program_id(1) == 0)
    def _():
        carry_ref[...] = jnp.zeros_like(carry_ref)

    x = x_ref[0]
    h = (x * g_ref[...]).astype(BF16)
    scale = _rms_scale(x)
    scale_t = jnp.broadcast_to(scale, (tm, LANES)).T[:F_ROWS]

    z = _dot_nt(wf_ref[...], h) * scale_t + bf_ref[...]
    logf = (jnp.minimum(z, 0.0) - jnp.log1p(jnp.exp(-jnp.abs(z)))) * LOG2E
    hi = logf.astype(BF16)
    r1 = logf - hi.astype(F32)
    mid = r1.astype(BF16)
    lo = (r1 - mid.astype(F32)).astype(BF16)
    row = lax.broadcasted_iota(jnp.int32, (tm, tm), 0)
    col = lax.broadcasted_iota(jnp.int32, (tm, tm), 1)
    upto = jnp.where(row <= col, 1.0, 0.0).astype(BF16)
    c3 = _dot(jnp.concatenate([hi, mid, lo], axis=0), upto)
    c_t = c3[:F_ROWS] + c3[F_ROWS:2 * F_ROWS] + c3[2 * F_ROWS:] + carry_ref[...]
    carry_ref[...] = c_t[:, tm - 1:tm]
    ct_ref[0] = c_t[:N_FOX_HEADS]
    c_pad = jnp.concatenate([c_t, jnp.zeros((LANES - F_ROWS, tm), F32)], axis=0)
    c_ref[0] = c_pad.T[:, :N_FOX_HEADS]

    for (_, dilation), w_ref, o_ref, y_ref in zip(DIL_PAIRS, (wa0_ref, wa1_ref, wa2_ref),
                                                  (a0_ref, a1_ref, a2_ref), (None,) + y_refs):
        y = _dot(h, w_ref[...]) * scale
        if dilation == 1:
            o_ref[0] = y.astype(BF16)
        else:
            for cb in range(DIL_QKV_WIDTH // LANES):
                y_ref[cb] = y[:, cb * LANES:(cb + 1) * LANES]
            for r in range(dilation):
                for cb in range(DIL_QKV_WIDTH // LANES):
                    c0 = r * DIL_QKV_WIDTH + cb * LANES
                    o_ref[0, :, c0:c0 + LANES] = (
                        y_ref[cb, pl.ds(r, tm // dilation, stride=dilation), :].astype(BF16))
    for w_ref, o_ref in ((wfq_ref, fq_ref), (wfk_ref, fk_ref), (wfv_ref, fv_ref)):
        o_ref[0] = (_dot(h, w_ref[...]) * scale).astype(BF16)
    gate_ref[0] = jax.nn.sigmoid(_dot(h, wg_ref[...]) * scale + bg_ref[...]).astype(BF16)


def _inproj(x, norm_g, w_in, b_fgt, b_gate, tm):
    b, s, _ = x.shape
    o2 = 3 * DIL_WIDTH
    o3 = o2 + 3 * FOX_WIDTH
    o4 = o3 + N_FOX_HEADS
    wq = w_in[:, :DIL_WIDTH] * (ATTN_SCALE * LOG2E)
    wk = w_in[:, DIL_WIDTH:2 * DIL_WIDTH]
    wv = w_in[:, 2 * DIL_WIDTH:o2]
    w = DIL_OUT_WIDTH
    wa = [jnp.concatenate([wq[:, g * w:(g + 1) * w], wk[:, g * w:(g + 1) * w],
                           wv[:, g * w:(g + 1) * w]], axis=1).astype(BF16)
          for g in range(N_DIL_GROUPS)]
    wfq = (w_in[:, o2:o2 + FOX_WIDTH] * (ATTN_SCALE * LOG2E)).astype(BF16)
    wfk = w_in[:, o2 + FOX_WIDTH:o2 + 2 * FOX_WIDTH].astype(BF16)
    wfv = w_in[:, o2 + 2 * FOX_WIDTH:o3].astype(BF16)
    wf = jnp.pad(w_in[:, o3:o4].T, ((0, F_ROWS - N_FOX_HEADS), (0, 0))).astype(BF16)
    bf = jnp.pad(b_fgt, (0, F_ROWS - N_FOX_HEADS)).reshape(F_ROWS, 1)
    wg = w_in[:, o4:].astype(BF16)
    bg = b_gate.reshape(1, -1)

    def const(shape):
        return _resident(shape, lambda bi, si: (0,) * len(shape))

    def tok(width, rows=tm):
        return pl.BlockSpec((1, rows, width), lambda bi, si: (bi, si, 0))

    dils = [d for _, d in DIL_PAIRS]
    assert all(tm % (16 * d) == 0 for d in dils)
    out_shape = [jax.ShapeDtypeStruct((b, s // d, d * DIL_QKV_WIDTH), BF16) for d in dils]
    out_shape += [jax.ShapeDtypeStruct((b, s, FOX_WIDTH), BF16)] * 3
    out_shape += [jax.ShapeDtypeStruct((b, s, N_FOX_HEADS), F32),
                  jax.ShapeDtypeStruct((b, N_FOX_HEADS, s), F32),
                  jax.ShapeDtypeStruct((b, s, 2 * D_MODEL), BF16)]
    out_specs = [tok(d * DIL_QKV_WIDTH, tm // d) for d in dils] + [tok(FOX_WIDTH)] * 3
    out_specs += [tok(N_FOX_HEADS),
                  pl.BlockSpec((1, N_FOX_HEADS, tm), lambda bi, si: (bi, 0, si)),
                  tok(2 * D_MODEL)]
    in_specs = [tok(D_MODEL), const((1, D_MODEL))]
    in_specs += [const((D_MODEL, DIL_QKV_WIDTH))] * 3 + [const((D_MODEL, FOX_WIDTH))] * 3
    in_specs += [const((F_ROWS, D_MODEL)), const((F_ROWS, 1)),
                 const((D_MODEL, 2 * D_MODEL)), const((1, 2 * D_MODEL))]
    return pl.pallas_call(
        _inproj_kernel,
        grid=(b, s // tm),
        in_specs=in_specs,
        out_specs=out_specs,
        out_shape=out_shape,
        scratch_shapes=[pltpu.VMEM((F_ROWS, 1), F32),
                        pltpu.VMEM((DIL_QKV_WIDTH // LANES, tm, LANES), F32),
                        pltpu.VMEM((DIL_QKV_WIDTH // LANES, tm, LANES), F32)],
        compiler_params=pltpu.CompilerParams(
            dimension_semantics=("arbitrary", "arbitrary"), vmem_limit_bytes=VMEM_LIMIT),
        name="inproj",
    )(x, norm_g.reshape(1, -1), *wa, wfq, wfk, wfv, wf, bf, wg, bg)


def _dil_kernel(cur_ref, prev_ref, o_ref, lse_ref, *, dilation, slopes, nsub, nres):
    w = DIL_OUT_WIDTH
    blk = DIL_BLOCK
    n = pl.program_id(1)
    row = lax.broadcasted_iota(jnp.int32, (blk, 2 * blk), 0)
    col = lax.broadcasted_iota(jnp.int32, (blk, 2 * blk), 1)
    rel = row - col + blk
    valid = jnp.logical_and(rel >= 0, rel <= blk)
    valid_first = jnp.logical_and(valid, jnp.logical_or(col >= blk, n > 0))
    dist = (rel * dilation).astype(F32)
    penalty = [[jnp.where(ok, (slope * LOG2E) * dist, -NEG_INF) for slope in slopes]
               for ok in (valid_first, valid)]
    lane = lax.broadcasted_iota(jnp.int32, (blk, LANES), 1)
    low = lane < HEAD_DIM

    for res in range(nres):
        c0 = res * 3 * w
        for sb in range(nsub):
            r0 = sb * blk
            for hp in range(HEADS_PER_DIL_GROUP // 2):
                lo_, hi_ = c0 + hp * LANES, c0 + (hp + 1) * LANES
                q_pair = cur_ref[0, r0:r0 + blk, lo_:hi_]
                q2 = jnp.concatenate([jnp.where(low, q_pair, jnp.zeros_like(q_pair)),
                                      jnp.where(low, jnp.zeros_like(q_pair), q_pair)], axis=0)
                if sb == 0:
                    k2 = jnp.concatenate([prev_ref[0, :, w + lo_:w + hi_],
                                          cur_ref[0, :blk, w + lo_:w + hi_]], axis=0)
                    v2 = jnp.concatenate([prev_ref[0, :, 2 * w + lo_:2 * w + hi_],
                                          cur_ref[0, :blk, 2 * w + lo_:2 * w + hi_]], axis=0)
                else:
                    k2 = cur_ref[0, r0 - blk:r0 + blk, w + lo_:w + hi_]
                    v2 = cur_ref[0, r0 - blk:r0 + blk, 2 * w + lo_:2 * w + hi_]
                s2 = _dot_nt(q2, k2)
                ps, ms = [], []
                for j in range(2):
                    s = s2[j * blk:(j + 1) * blk] - penalty[min(sb, 1)][2 * hp + j]
                    m = jnp.max(s, axis=-1, keepdims=True)
                    ps.append(jnp.exp2(s - m).astype(BF16))
                    ms.append(jnp.broadcast_to(m, (blk, LANES)))
                acc2 = _dot(jnp.concatenate(ps, axis=0),
                            jnp.concatenate([v2, jnp.ones_like(v2)], axis=1))
                den = acc2[:, LANES:]
                out = acc2[:, :LANES] / den
                lse = jnp.concatenate(ms, axis=0) + jnp.log2(den)
                tokens = pl.ds(r0 * dilation + res, blk, stride=dilation)
                o_ref[0, hp, tokens, :] = jnp.where(low, out[:blk], out[blk:])
                lse_ref[0, hp, tokens, :] = jnp.where(low, lse[:blk], lse[blk:])


def _dilated(view, group):
    window, dilation = DIL_PAIRS[group]
    assert window // dilation == DIL_BLOCK
    w = DIL_OUT_WIDTH
    batch, rows, _ = view.shape
    qrows = min(rows, 4 * DIL_BLOCK)
    nsub = qrows // DIL_BLOCK
    span = qrows * dilation
    slopes = tuple(float(2.0 ** (-8.0 * (group * HEADS_PER_DIL_GROUP + h + 1) / N_DIL_HEADS))
                   for h in range(HEADS_PER_DIL_GROUP))
    planes = jax.ShapeDtypeStruct((batch, w // LANES, rows * dilation, LANES), F32)
    plane_spec = pl.BlockSpec((1, w // LANES, span, LANES), lambda b, n: (b, 0, n, 0))
    return pl.pallas_call(
        functools.partial(_dil_kernel, dilation=dilation, slopes=slopes, nsub=nsub,
                          nres=dilation),
        grid=(batch, rows // qrows),
        in_specs=[pl.BlockSpec((1, qrows, dilation * 3 * w), lambda b, n: (b, n, 0)),
                  pl.BlockSpec((1, DIL_BLOCK, dilation * 3 * w),
                               lambda b, n: (b, jnp.maximum(n * nsub - 1, 0), 0))],
        out_specs=[plane_spec, plane_spec],
        out_shape=[planes, planes],
        compiler_params=pltpu.CompilerParams(
            dimension_semantics=("arbitrary", "arbitrary"), vmem_limit_bytes=VMEM_LIMIT),
        name=f"dilated{group}",
    )(view, view)


def _fox_kernel(q_ref, k_ref, v_ref, c_ref, ct_ref, o_ref,
                sa_ref, mxa_ref, sb_ref, mxb_ref, sc_ref, mxc_ref,
                m0_ref, acc0_ref, cq0_ref, m1_ref, acc1_ref, cq1_ref, v1_ref, *, tq):
    hp = pl.program_id(1)
    j = pl.program_id(2)
    n_qtiles = k_ref.shape[1] // tq
    lane = lax.broadcasted_iota(jnp.int32, (tq, LANES), 1)
    low = lane < HEAD_DIM
    head_lane = lax.broadcasted_iota(jnp.int32, (tq, N_FOX_HEADS), 1)
    row = lax.broadcasted_iota(jnp.int32, (tq, tq), 0)
    col = lax.broadcasted_iota(jnp.int32, (tq, tq), 1)
    causal = col <= row

    @pl.when(j == 0)
    def _():
        v1_ref[:, :LANES] = v_ref[0]
        v1_ref[:, LANES:] = jnp.ones((v1_ref.shape[0], LANES), BF16)

    bufs = ((sa_ref, mxa_ref), (sb_ref, mxb_ref), (sc_ref, mxc_ref))
    states = ((m0_ref, acc0_ref, cq0_ref), (m1_ref, acc1_ref, cq1_ref))

    class QTile:
        def __init__(self, qi, state):
            self.qi = qi
            self.m, self.acc, self.cq = state
            self.rows = slice(qi * tq, (qi + 1) * tq)
            q_pair = q_ref[0, self.rows, :]
            self.q2 = jnp.concatenate([jnp.where(low, q_pair, jnp.zeros_like(q_pair)),
                                       jnp.where(low, jnp.zeros_like(q_pair), q_pair)], axis=0)
            c_blk = c_ref[0, self.rows, :]
            for h in range(2):
                self.cq[h * tq:(h + 1) * tq, :] = jnp.broadcast_to(
                    jnp.sum(jnp.where(head_lane == 2 * hp + h, c_blk, 0.0),
                            axis=-1, keepdims=True), (tq, LANES))
            self.m[...] = jnp.full_like(self.m, NEG_INF)
            self.acc[...] = jnp.zeros_like(self.acc)

        def produce(self, kt, buf):
            s_ref, mx_ref = buf
            keys = slice(kt * tq, (kt + 1) * tq)
            s2 = _dot_nt(self.q2, k_ref[0, keys, :])
            for h in range(2):
                rows = slice(h * tq, (h + 1) * tq)
                s = s2[rows] - ct_ref[0, pl.ds(2 * hp + h, 1), keys]
                if kt == self.qi:
                    s = jnp.where(causal, s, NEG_INF)
                s_ref[rows, :] = s
                mx_ref[rows, :] = jnp.broadcast_to(
                    jnp.max(s, axis=-1, keepdims=True), (tq, LANES)) + self.cq[rows, :]

        def consume(self, kt, buf):
            s_ref, mx_ref = buf
            m = self.m[...]
            m_new = jnp.maximum(m, mx_ref[...])
            alpha = jnp.exp2(m - m_new)
            p = jnp.exp2(s_ref[...] - jnp.tile(m_new - self.cq[...], (1, tq // LANES)))
            self.acc[...] = (jnp.tile(alpha, (1, 2)) * self.acc[...]
                             + _dot(p.astype(BF16), v1_ref[kt * tq:(kt + 1) * tq, :]))
            self.m[...] = m_new

        def finish(self):
            out = self.acc[:, :LANES] / self.acc[:, LANES:]
            o_ref[0, self.rows, :] = jnp.where(low, out[:tq], out[tq:]).astype(o_ref.dtype)

    def schedule(step):
        work = []
        for qi, state in zip((step, n_qtiles - 1 - step), states):
            qt = QTile(qi, state)
            work += [(qt, kt) for kt in [qi] + list(range(qi))]
        ahead = len(bufs) - 1
        for n in range(min(ahead, len(work))):
            qt, kt = work[n]
            qt.produce(kt, bufs[n])
        for n, (qt, kt) in enumerate(work):
            qt.consume(kt, bufs[n % len(bufs)])
            if n + 1 == len(work) or work[n + 1][0] is not qt:
                qt.finish()
            if n + ahead < len(work):
                qt2, kt2 = work[n + ahead]
                qt2.produce(kt2, bufs[(n + ahead) % len(bufs)])

    assert n_qtiles % 2 == 0
    for step in range(n_qtiles // 2):
        pl.when(j == step)(functools.partial(schedule, step))


def _fox(fq, fk, fv, c, c_t, tq):
    b, s, _ = fq.shape
    rows = 2 * tq
    n_qtiles = s // tq

    def whole(width):
        return pl.BlockSpec((1, s, width), lambda bi, hp, j: (bi, 0, hp))

    return pl.pallas_call(
        functools.partial(_fox_kernel, tq=tq),
        grid=(b, N_FOX_HEADS // 2, n_qtiles // 2),
        in_specs=[whole(LANES), whole(LANES), whole(LANES),
                  pl.BlockSpec((1, s, N_FOX_HEADS), lambda bi, hp, j: (bi, 0, 0)),
                  pl.BlockSpec((1, N_FOX_HEADS, s), lambda bi, hp, j: (bi, 0, 0))],
        out_specs=whole(LANES),
        out_shape=jax.ShapeDtypeStruct((b, s, FOX_WIDTH), BF16),
        scratch_shapes=[pltpu.VMEM((rows, tq), F32),
                        pltpu.VMEM((rows, LANES), F32)] * 3
        + [pltpu.VMEM((rows, LANES), F32),
           pltpu.VMEM((rows, 2 * LANES), F32),
           pltpu.VMEM((rows, LANES), F32)] * 2
        + [pltpu.VMEM((s, 2 * LANES), BF16)],
        compiler_params=pltpu.CompilerParams(
            dimension_semantics=("arbitrary", "arbitrary", "arbitrary"),
            vmem_limit_bytes=VMEM_LIMIT),
        name="fox",
    )(fq, fk, fv, c, c_t)


FF_CHUNKS = ((0, 512), (512, 1024), (1024, 1536), (1536, 2048), (2048, 2560), (2560, D_FF))


def _mix_ffn_kernel(x_ref, o0_ref, o1_ref, o2_ref, l0_ref, l1_ref, l2_ref, ob_ref, gate_ref,
                    wd_ref, wfo_ref, wo_ref, g2_ref, wgi_ref, wui_ref, wdn_ref, gfin_ref,
                    out_ref):
    def planes(ref):
        return jnp.concatenate([ref[0, p] for p in range(ref.shape[1])], axis=1)

    o0, o1, o2 = planes(o0_ref), planes(o1_ref), planes(o2_ref)
    l0, l1, l2 = planes(l0_ref), planes(l1_ref), planes(l2_ref)
    m = jnp.maximum(jnp.maximum(l0, l1), l2)
    e0, e1, e2 = jnp.exp2(l0 - m), jnp.exp2(l1 - m), jnp.exp2(l2 - m)
    o_a = (e0 * o0 + e1 * o1 + e2 * o2) / (e0 + e1 + e2)
    y_a = _dot(o_a.astype(BF16), wd_ref[...])
    y_b = _dot(ob_ref[...], wfo_ref[...])
    merged = (gate_ref[:, :D_MODEL].astype(F32) * y_a
              + gate_ref[:, D_MODEL:].astype(F32) * y_b)
    x1 = x_ref[...] + _dot(merged.astype(BF16), wo_ref[...])

    h2 = (x1 * g2_ref[...]).astype(BF16)
    scale = _rms_scale(x1)
    acc = jnp.zeros_like(x1)
    for c0, c1 in FF_CHUNKS:
        gte = _dot(h2, wgi_ref[:, c0:c1]) * scale
        up = _dot(h2, wui_ref[:, c0:c1]) * scale
        act = gte * jax.nn.sigmoid(gte) * up
        acc = acc + _dot(act.astype(BF16), wdn_ref[c0:c1, :])
    out_ref[...] = _rms(x1 + acc, gfin_ref[...])


def _mix_ffn(x2d, o_g, lse_g, o_b, gates, w_dil_out, w_fox_out, w_out, norm_ffn_g,
             w_ffn_in, w_ffn_down, norm_final_g, tm):
    n = x2d.shape[0]
    w = DIL_OUT_WIDTH
    batch, n_planes, seq, _ = o_g[0].shape
    tiles = seq // tm

    def const(shape):
        return _resident(shape, lambda i: (0,) * len(shape))

    def tok(width):
        return pl.BlockSpec((tm, width), lambda i: (i, 0))

    dil_specs = [pl.BlockSpec((1, n_planes, tm, LANES),
                              lambda i: (i // tiles, 0, i % tiles, 0))] * N_DIL_GROUPS
    in_specs = [tok(D_MODEL)] + dil_specs + dil_specs + [tok(FOX_WIDTH), tok(2 * D_MODEL)]
    in_specs += [const((w, D_MODEL)), const((FOX_WIDTH, D_MODEL)),
                 const((D_MODEL, D_MODEL)), const((1, D_MODEL)),
                 const((D_MODEL, D_FF)), const((D_MODEL, D_FF)), const((D_FF, D_MODEL)),
                 const((1, D_MODEL))]
    return pl.pallas_call(
        _mix_ffn_kernel,
        grid=(n // tm,),
        in_specs=in_specs,
        out_specs=tok(D_MODEL),
        out_shape=jax.ShapeDtypeStruct((n, D_MODEL), F32),
        compiler_params=pltpu.CompilerParams(
            dimension_semantics=("arbitrary",), vmem_limit_bytes=VMEM_LIMIT),
        name="mix_ffn",
    )(x2d, *o_g, *lse_g, o_b, gates,
      w_dil_out.astype(BF16), w_fox_out.astype(BF16), w_out.astype(BF16),
      norm_ffn_g.reshape(1, -1),
      w_ffn_in[:, :D_FF].astype(BF16), w_ffn_in[:, D_FF:].astype(BF16),
      w_ffn_down.astype(BF16), norm_final_g.reshape(1, -1))


def kernel(x, norm_mix_g, w_in, b_fgt, b_gate, w_dil_out, w_fox_out, w_out, norm_ffn_g,
           w_ffn_in, w_ffn_down, norm_final_g):
    b, s, d = x.shape
    assert w_in.shape[0] == 1
    layer = 0
    a0, a1, a2, fq, fk, fv, c, c_t, gates = _inproj(
        x, norm_mix_g[layer], w_in[layer], b_fgt[layer], b_gate[layer], tm=512)
    o_g, lse_g = zip(*[_dilated(a, g) for g, a in enumerate((a0, a1, a2))])
    o_b = _fox(fq, fk, fv, c, c_t, tq=512)
    out = _mix_ffn(x.reshape(b * s, d), o_g, lse_g, o_b.reshape(b * s, FOX_WIDTH),
                   gates.reshape(b * s, 2 * D_MODEL), w_dil_out[layer], w_fox_out[layer],
                   w_out[layer], norm_ffn_g[layer], w_ffn_in[layer], w_ffn_down[layer],
                   norm_final_g, tm=512)
    return out.reshape(b, s, d)
```

```python
import functools

import jax
import jax.numpy as jnp
from jax import lax
from jax.experimental import pallas as pl
from jax.experimental.pallas import tpu as pltpu

D_MODEL = 1024
HEAD_DIM = 64
DIL_PAIRS = ((128, 1), (512, 4), (2048, 16))
N_DIL_GROUPS = len(DIL_PAIRS)
HEADS_PER_DIL_GROUP = 4
N_DIL_HEADS = N_DIL_GROUPS * HEADS_PER_DIL_GROUP
DIL_OUT_WIDTH = HEADS_PER_DIL_GROUP * HEAD_DIM
DIL_QKV_WIDTH = 3 * DIL_OUT_WIDTH
DIL_WIDTH = N_DIL_HEADS * HEAD_DIM
N_FOX_HEADS = 8
FOX_WIDTH = N_FOX_HEADS * HEAD_DIM
D_FF = 2816
RMS_EPS = 1e-6
NEG_INF = -1e30
ATTN_SCALE = HEAD_DIM ** -0.5
LOG2E = 1.4426950408889634

LANES = 128
DIL_BLOCK = 128
DIL_SPAN = 2048
F_ROWS = 16
VMEM_LIMIT = 56 * 1024 * 1024

F32 = jnp.float32
BF16 = jnp.bfloat16


def _dot(a, b):
    return jnp.dot(a, b, preferred_element_type=F32)


def _dot_nt(a, b):
    return lax.dot_general(a, b, (((1,), (1,)), ((), ())), preferred_element_type=F32)


def _rms_scale(x):
    return lax.rsqrt(jnp.mean(x * x, axis=-1, keepdims=True) + RMS_EPS)


def _rms(x, g):
    return x * _rms_scale(x) * g


def _resident(shape, index_map):
    return pl.BlockSpec(shape, index_map, pipeline_mode=pl.Buffered(1))


def _inproj_kernel(x_ref, g_ref, wa0_ref, wa1_ref, wa2_ref, wfq_ref, wfk_ref, wfv_ref,
                   wf_ref, bf_ref, wg_ref, bg_ref,
                   a0_ref, a1_ref, a2_ref, fq_ref, fk_ref, fv_ref, c_ref, ct_ref, gate_ref,
                   carry_ref, *y_refs):
    tm = x_ref.shape[1]

    @pl.when(pl.program_id(1) == 0)
    def _():
        carry_ref[...] = jnp.zeros_like(carry_ref)

    x = x_ref[0]
    h = (x * g_ref[...]).astype(BF16)
    scale = _rms_scale(x)
    scale_t = jnp.broadcast_to(scale, (tm, LANES)).T[:F_ROWS]

    z = _dot_nt(wf_ref[...], h) * scale_t + bf_ref[...]
    logf = (jnp.minimum(z, 0.0) - jnp.log1p(jnp.exp(-jnp.abs(z)))) * LOG2E
    hi = logf.astype(BF16)
    r1 = logf - hi.astype(F32)
    mid = r1.astype(BF16)
    lo = (r1 - mid.astype(F32)).astype(BF16)
    row = lax.broadcasted_iota(jnp.int32, (tm, tm), 0)
    col = lax.broadcasted_iota(jnp.int32, (tm, tm), 1)
    upto = jnp.where(row <= col, 1.0, 0.0).astype(BF16)
    c3 = _dot(jnp.concatenate([hi, mid, lo], axis=0), upto)
    c_t = c3[:F_ROWS] + c3[F_ROWS:2 * F_ROWS] + c3[2 * F_ROWS:] + carry_ref[...]
    carry_ref[...] = c_t[:, tm - 1:tm]
    ct_ref[0] = c_t[:N_FOX_HEADS]
    c_pad = jnp.concatenate([c_t, jnp.zeros((LANES - F_ROWS, tm), F32)], axis=0)
    c_ref[0] = c_pad.T[:, :N_FOX_HEADS]

    for (_, dilation), w_ref, o_ref, y_ref in zip(DIL_PAIRS, (wa0_ref, wa1_ref, wa2_ref),
                                                  (a0_ref, a1_ref, a2_ref), (None,) + y_refs):
        y = _dot(h, w_ref[...]) * scale
        if dilation == 1:
            o_ref[0] = y.astype(BF16)
        else:
            for cb in range(DIL_QKV_WIDTH // LANES):
                y_ref[cb] = y[:, cb * LANES:(cb + 1) * LANES]
            for r in range(dilation):
                for cb in range(DIL_QKV_WIDTH // LANES):
                    c0 = r * DIL_QKV_WIDTH + cb * LANES
                    o_ref[0, :, c0:c0 + LANES] = (
                        y_ref[cb, pl.ds(r, tm // dilation, stride=dilation), :].astype(BF16))
    for w_ref, o_ref in ((wfq_ref, fq_ref), (wfk_ref, fk_ref), (wfv_ref, fv_ref)):
        o_ref[0] = (_dot(h, w_ref[...]) * scale).astype(BF16)
    gate_ref[0] = jax.nn.sigmoid(_dot(h, wg_ref[...]) * scale + bg_ref[...]).astype(BF16)


def _inproj(x, norm_g, w_in, b_fgt, b_gate, tm):
    b, s, _ = x.shape
    o2 = 3 * DIL_WIDTH
    o3 = o2 + 3 * FOX_WIDTH
    o4 = o3 + N_FOX_HEADS
    wq = w_in[:, :DIL_WIDTH] * (ATTN_SCALE * LOG2E)
    wk = w_in[:, DIL_WIDTH:2 * DIL_WIDTH]
    wv = w_in[:, 2 * DIL_WIDTH:o2]
    w = DIL_OUT_WIDTH
    wa = [jnp.concatenate([wq[:, g * w:(g + 1) * w], wk[:, g * w:(g + 1) * w],
                           wv[:, g * w:(g + 1) * w]], axis=1).astype(BF16)
          for g in range(N_DIL_GROUPS)]
    wfq = (w_in[:, o2:o2 + FOX_WIDTH] * (ATTN_SCALE * LOG2E)).astype(BF16)
    wfk = w_in[:, o2 + FOX_WIDTH:o2 + 2 * FOX_WIDTH].astype(BF16)
    wfv = w_in[:, o2 + 2 * FOX_WIDTH:o3].astype(BF16)
    wf = jnp.pad(w_in[:, o3:o4].T, ((0, F_ROWS - N_FOX_HEADS), (0, 0))).astype(BF16)
    bf = jnp.pad(b_fgt, (0, F_ROWS - N_FOX_HEADS)).reshape(F_ROWS, 1)
    wg = w_in[:, o4:].astype(BF16)
    bg = b_gate.reshape(1, -1)

    def const(shape):
        return _resident(shape, lambda bi, si: (0,) * len(shape))

    def tok(width, rows=tm):
        return pl.BlockSpec((1, rows, width), lambda bi, si: (bi, si, 0))

    dils = [d for _, d in DIL_PAIRS]
    assert all(tm % (16 * d) == 0 for d in dils)
    out_shape = [jax.ShapeDtypeStruct((b, s // d, d * DIL_QKV_WIDTH), BF16) for d in dils]
    out_shape += [jax.ShapeDtypeStruct((b, s, FOX_WIDTH), BF16)] * 3
    out_shape += [jax.ShapeDtypeStruct((b, s, N_FOX_HEADS), F32),
                  jax.ShapeDtypeStruct((b, N_FOX_HEADS, s), F32),
                  jax.ShapeDtypeStruct((b, s, 2 * D_MODEL), BF16)]
    out_specs = [tok(d * DIL_QKV_WIDTH, tm // d) for d in dils] + [tok(FOX_WIDTH)] * 3
    out_specs += [tok(N_FOX_HEADS),
                  pl.BlockSpec((1, N_FOX_HEADS, tm), lambda bi, si: (bi, 0, si)),
                  tok(2 * D_MODEL)]
    in_specs = [tok(D_MODEL), const((1, D_MODEL))]
    in_specs += [const((D_MODEL, DIL_QKV_WIDTH))] * 3 + [const((D_MODEL, FOX_WIDTH))] * 3
    in_specs += [const((F_ROWS, D_MODEL)), const((F_ROWS, 1)),
                 const((D_MODEL, 2 * D_MODEL)), const((1, 2 * D_MODEL))]
    return pl.pallas_call(
        _inproj_kernel,
        grid=(b, s // tm),
        in_specs=in_specs,
        out_specs=out_specs,
        out_shape=out_shape,
        scratch_shapes=[pltpu.VMEM((F_ROWS, 1), F32),
                        pltpu.VMEM((DIL_QKV_WIDTH // LANES, tm, LANES), F32),
                        pltpu.VMEM((DIL_QKV_WIDTH // LANES, tm, LANES), F32)],
        compiler_params=pltpu.CompilerParams(
            dimension_semantics=("arbitrary", "arbitrary"), vmem_limit_bytes=VMEM_LIMIT),
        name="inproj",
    )(x, norm_g.reshape(1, -1), *wa, wfq, wfk, wfv, wf, bf, wg, bg)


def _dil_kernel(cur_ref, prev_ref, o_ref, lse_ref, *, dilation, slopes, nsub, nres):
    w = DIL_OUT_WIDTH
    blk = DIL_BLOCK
    n = pl.program_id(1)
    row = lax.broadcasted_iota(jnp.int32, (blk, 2 * blk), 0)
    col = lax.broadcasted_iota(jnp.int32, (blk, 2 * blk), 1)
    rel = row - col + blk
    valid = jnp.logical_and(rel >= 0, rel <= blk)
    valid_first = jnp.logical_and(valid, jnp.logical_or(col >= blk, n > 0))
    dist = (rel * dilation).astype(F32)
    penalty = [[jnp.where(ok, (slope * LOG2E) * dist, -NEG_INF) for slope in slopes]
               for ok in (valid_first, valid)]
    lane = lax.broadcasted_iota(jnp.int32, (blk, LANES), 1)
    low = lane < HEAD_DIM

    for res in range(nres):
        c0 = res * 3 * w
        for sb in range(nsub):
            r0 = sb * blk
            for hp in range(HEADS_PER_DIL_GROUP // 2):
                lo_, hi_ = c0 + hp * LANES, c0 + (hp + 1) * LANES
                q_pair = cur_ref[0, r0:r0 + blk, lo_:hi_]
                q2 = jnp.concatenate([jnp.where(low, q_pair, jnp.zeros_like(q_pair)),
                                      jnp.where(low, jnp.zeros_like(q_pair), q_pair)], axis=0)
                if sb == 0:
                    k2 = jnp.concatenate([prev_ref[0, :, w + lo_:w + hi_],
                                          cur_ref[0, :blk, w + lo_:w + hi_]], axis=0)
                    v2 = jnp.concatenate([prev_ref[0, :, 2 * w + lo_:2 * w + hi_],
                                          cur_ref[0, :blk, 2 * w + lo_:2 * w + hi_]], axis=0)
                else:
                    k2 = cur_ref[0, r0 - blk:r0 + blk, w + lo_:w + hi_]
                    v2 = cur_ref[0, r0 - blk:r0 + blk, 2 * w + lo_:2 * w + hi_]
                s2 = _dot_nt(q2, k2)
                ps, ms = [], []
                for j in range(2):
                    s = s2[j * blk:(j + 1) * blk] - penalty[min(sb, 1)][2 * hp + j]
                    m = jnp.max(s, axis=-1, keepdims=True)
                    ps.append(jnp.exp2(s - m).astype(BF16))
                    ms.append(jnp.broadcast_to(m, (blk, LANES)))
                acc2 = _dot(jnp.concatenate(ps, axis=0),
                            jnp.concatenate([v2, jnp.ones_like(v2)], axis=1))
                den = acc2[:, LANES:]
                out = acc2[:, :LANES] / den
                lse = jnp.concatenate(ms, axis=0) + jnp.log2(den)
                tokens = pl.ds(r0 * dilation + res, blk, stride=dilation)
                o_ref[0, hp, tokens, :] = jnp.where(low, out[:blk], out[blk:])
                lse_ref[0, hp, tokens, :] = jnp.where(low, lse[:blk], lse[blk:])


def _dilated(view, group):
    window, dilation = DIL_PAIRS[group]
    assert window // dilation == DIL_BLOCK
    w = DIL_OUT_WIDTH
    batch, rows, _ = view.shape
    qrows = min(rows, max(DIL_SPAN // dilation, 2 * DIL_BLOCK))
    nsub = qrows // DIL_BLOCK
    span = qrows * dilation
    slopes = tuple(float(2.0 ** (-8.0 * (group * HEADS_PER_DIL_GROUP + h + 1) / N_DIL_HEADS))
                   for h in range(HEADS_PER_DIL_GROUP))
    planes = jax.ShapeDtypeStruct((batch, w // LANES, rows * dilation, LANES), F32)
    plane_spec = pl.BlockSpec((1, w // LANES, span, LANES), lambda b, n: (b, 0, n, 0))
    return pl.pallas_call(
        functools.partial(_dil_kernel, dilation=dilation, slopes=slopes, nsub=nsub,
                          nres=dilation),
        grid=(batch, rows // qrows),
        in_specs=[pl.BlockSpec((1, qrows, dilation * 3 * w), lambda b, n: (b, n, 0)),
                  pl.BlockSpec((1, DIL_BLOCK, dilation * 3 * w),
                               lambda b, n: (b, jnp.maximum(n * nsub - 1, 0), 0))],
        out_specs=[plane_spec, plane_spec],
        out_shape=[planes, planes],
        compiler_params=pltpu.CompilerParams(
            dimension_semantics=("arbitrary", "arbitrary"), vmem_limit_bytes=VMEM_LIMIT),
        name=f"dilated{group}",
    )(view, view)


def _fox_kernel(q_ref, k_ref, v_ref, c_ref, ct_ref, o_ref,
                sa_ref, mxa_ref, sb_ref, mxb_ref, sc_ref, mxc_ref,
                m0_ref, acc0_ref, cq0_ref, m1_ref, acc1_ref, cq1_ref, v1_ref, *, tq):
    hp = pl.program_id(1)
    j = pl.program_id(2)
    n_qtiles = k_ref.shape[1] // tq
    lane = lax.broadcasted_iota(jnp.int32, (tq, LANES), 1)
    low = lane < HEAD_DIM
    head_lane = lax.broadcasted_iota(jnp.int32, (tq, N_FOX_HEADS), 1)
    row = lax.broadcasted_iota(jnp.int32, (tq, tq), 0)
    col = lax.broadcasted_iota(jnp.int32, (tq, tq), 1)
    causal = col <= row

    @pl.when(j == 0)
    def _():
        v1_ref[:, :LANES] = v_ref[0]
        v1_ref[:, LANES:] = jnp.ones((v1_ref.shape[0], LANES), BF16)

    bufs = ((sa_ref, mxa_ref), (sb_ref, mxb_ref), (sc_ref, mxc_ref))
    states = ((m0_ref, acc0_ref, cq0_ref), (m1_ref, acc1_ref, cq1_ref))

    class QTile:
        def __init__(self, qi, state):
            self.qi = qi
            self.m, self.acc, self.cq = state
            self.rows = slice(qi * tq, (qi + 1) * tq)
            q_pair = q_ref[0, self.rows, :]
            self.q2 = jnp.concatenate([jnp.where(low, q_pair, jnp.zeros_like(q_pair)),
                                       jnp.where(low, jnp.zeros_like(q_pair), q_pair)], axis=0)
            c_blk = c_ref[0, self.rows, :]
            for h in range(2):
                self.cq[h * tq:(h + 1) * tq, :] = jnp.broadcast_to(
                    jnp.sum(jnp.where(head_lane == 2 * hp + h, c_blk, 0.0),
                            axis=-1, keepdims=True), (tq, LANES))
            self.m[...] = jnp.full_like(self.m, NEG_INF)
            self.acc[...] = jnp.zeros_like(self.acc)

        def produce(self, kt, buf):
            s_ref, mx_ref = buf
            keys = slice(kt * tq, (kt + 1) * tq)
            s2 = _dot_nt(self.q2, k_ref[0, keys, :])
            for h in range(2):
                rows = slice(h * tq, (h + 1) * tq)
                s = s2[rows] - ct_ref[0, pl.ds(2 * hp + h, 1), keys]
                if kt == self.qi:
                    s = jnp.where(causal, s, NEG_INF)
                s_ref[rows, :] = s
                mx_ref[rows, :] = jnp.broadcast_to(
                    jnp.max(s, axis=-1, keepdims=True), (tq, LANES)) + self.cq[rows, :]

        def consume(self, kt, buf):
            s_ref, mx_ref = buf
            m = self.m[...]
            m_new = jnp.maximum(m, mx_ref[...])
            alpha = jnp.exp2(m - m_new)
            p = jnp.exp2(s_ref[...] - jnp.tile(m_new - self.cq[...], (1, tq // LANES)))
            self.acc[...] = (jnp.tile(alpha, (1, 2)) * self.acc[...]
                             + _dot(p.astype(BF16), v1_ref[kt * tq:(kt + 1) * tq, :]))
            self.m[...] = m_new

        def finish(self):
            out = self.acc[:, :LANES] / self.acc[:, LANES:]
            o_ref[0, self.rows, :] = jnp.where(low, out[:tq], out[tq:]).astype(o_ref.dtype)

    def schedule(step):
        work = []
        for qi, state in zip((step, n_qtiles - 1 - step), states):
            qt = QTile(qi, state)
            work += [(qt, kt) for kt in [qi] + list(range(qi))]
        ahead = len(bufs) - 1
        for n in range(min(ahead, len(work))):
            qt, kt = work[n]
            qt.produce(kt, bufs[n])
        for n, (qt, kt) in enumerate(work):
            qt.consume(kt, bufs[n % len(bufs)])
            if n + 1 == len(work) or work[n + 1][0] is not qt:
                qt.finish()
            if n + ahead < len(work):
                qt2, kt2 = work[n + ahead]
                qt2.produce(kt2, bufs[(n + ahead) % len(bufs)])

    assert n_qtiles % 2 == 0
    for step in range(n_qtiles // 2):
        pl.when(j == step)(functools.partial(schedule, step))


def _fox(fq, fk, fv, c, c_t, tq):
    b, s, _ = fq.shape
    rows = 2 * tq
    n_qtiles = s // tq

    def whole(width):
        return pl.BlockSpec((1, s, width), lambda bi, hp, j: (bi, 0, hp))

    return pl.pallas_call(
        functools.partial(_fox_kernel, tq=tq),
        grid=(b, N_FOX_HEADS // 2, n_qtiles // 2),
        in_specs=[whole(LANES), whole(LANES), whole(LANES),
                  pl.BlockSpec((1, s, N_FOX_HEADS), lambda bi, hp, j: (bi, 0, 0)),
                  pl.BlockSpec((1, N_FOX_HEADS, s), lambda bi, hp, j: (bi, 0, 0))],
        out_specs=whole(LANES),
        out_shape=jax.ShapeDtypeStruct((b, s, FOX_WIDTH), BF16),
        scratch_shapes=[pltpu.VMEM((rows, tq), F32),
                        pltpu.VMEM((rows, LANES), F32)] * 3
        + [pltpu.VMEM((rows, LANES), F32),
           pltpu.VMEM((rows, 2 * LANES), F32),
           pltpu.VMEM((rows, LANES), F32)] * 2
        + [pltpu.VMEM((s, 2 * LANES), BF16)],
        compiler_params=pltpu.CompilerParams(
            dimension_semantics=("arbitrary", "arbitrary", "arbitrary"),
            vmem_limit_bytes=VMEM_LIMIT),
        name="fox",
    )(fq, fk, fv, c, c_t)


FF_CHUNKS = ((0, 512), (512, 1024), (1024, 1536), (1536, 2048), (2048, 2560), (2560, D_FF))


def _mix_ffn_kernel(x_ref, o0_ref, o1_ref, o2_ref, l0_ref, l1_ref, l2_ref, ob_ref, gate_ref,
                    wd_ref, wfo_ref, wo_ref, g2_ref, wgi_ref, wui_ref, wdn_ref, gfin_ref,
                    out_ref):
    def planes(ref):
        return jnp.concatenate([ref[0, p] for p in range(ref.shape[1])], axis=1)

    o0, o1, o2 = planes(o0_ref), planes(o1_ref), planes(o2_ref)
    l0, l1, l2 = planes(l0_ref), planes(l1_ref), planes(l2_ref)
    m = jnp.maximum(jnp.maximum(l0, l1), l2)
    e0, e1, e2 = jnp.exp2(l0 - m), jnp.exp2(l1 - m), jnp.exp2(l2 - m)
    o_a = (e0 * o0 + e1 * o1 + e2 * o2) / (e0 + e1 + e2)
    y_a = _dot(o_a.astype(BF16), wd_ref[...])
    y_b = _dot(ob_ref[...], wfo_ref[...])
    merged = (gate_ref[:, :D_MODEL].astype(F32) * y_a
              + gate_ref[:, D_MODEL:].astype(F32) * y_b)
    x1 = x_ref[...] + _dot(merged.astype(BF16), wo_ref[...])

    h2 = (x1 * g2_ref[...]).astype(BF16)
    scale = _rms_scale(x1)
    acc = jnp.zeros_like(x1)
    for c0, c1 in FF_CHUNKS:
        gte = _dot(h2, wgi_ref[:, c0:c1]) * scale
        up = _dot(h2, wui_ref[:, c0:c1]) * scale
        act = gte * jax.nn.sigmoid(gte) * up
        acc = acc + _dot(act.astype(BF16), wdn_ref[c0:c1, :])
    out_ref[...] = _rms(x1 + acc, gfin_ref[...])


def _mix_ffn(x2d, o_g, lse_g, o_b, gates, w_dil_out, w_fox_out, w_out, norm_ffn_g,
             w_ffn_in, w_ffn_down, norm_final_g, tm):
    n = x2d.shape[0]
    w = DIL_OUT_WIDTH
    batch, n_planes, seq, _ = o_g[0].shape
    tiles = seq // tm

    def const(shape):
        return _resident(shape, lambda i: (0,) * len(shape))

    def tok(width):
        return pl.BlockSpec((tm, width), lambda i: (i, 0))

    dil_specs = [pl.BlockSpec((1, n_planes, tm, LANES),
                              lambda i: (i // tiles, 0, i % tiles, 0))] * N_DIL_GROUPS
    in_specs = [tok(D_MODEL)] + dil_specs + dil_specs + [tok(FOX_WIDTH), tok(2 * D_MODEL)]
    in_specs += [const((w, D_MODEL)), const((FOX_WIDTH, D_MODEL)),
                 const((D_MODEL, D_MODEL)), const((1, D_MODEL)),
                 const((D_MODEL, D_FF)), const((D_MODEL, D_FF)), const((D_FF, D_MODEL)),
                 const((1, D_MODEL))]
    return pl.pallas_call(
        _mix_ffn_kernel,
        grid=(n // tm,),
        in_specs=in_specs,
        out_specs=tok(D_MODEL),
        out_shape=jax.ShapeDtypeStruct((n, D_MODEL), F32),
        compiler_params=pltpu.CompilerParams(
            dimension_semantics=("arbitrary",), vmem_limit_bytes=VMEM_LIMIT),
        name="mix_ffn",
    )(x2d, *o_g, *lse_g, o_b, gates,
      w_dil_out.astype(BF16), w_fox_out.astype(BF16), w_out.astype(BF16),
      norm_ffn_g.reshape(1, -1),
      w_ffn_in[:, :D_FF].astype(BF16), w_ffn_in[:, D_FF:].astype(BF16),
      w_ffn_down.astype(BF16), norm_final_g.reshape(1, -1))


def kernel(x, norm_mix_g, w_in, b_fgt, b_gate, w_dil_out, w_fox_out, w_out, norm_ffn_g,
           w_ffn_in, w_ffn_down, norm_final_g):
    b, s, d = x.shape
    assert w_in.shape[0] == 1
    layer = 0
    a0, a1, a2, fq, fk, fv, c, c_t, gates = _inproj(
        x, norm_mix_g[layer], w_in[layer], b_fgt[layer], b_gate[layer], tm=512)
    o_g, lse_g = zip(*[_dilated(a, g) for g, a in enumerate((a0, a1, a2))])
    o_b = _fox(fq, fk, fv, c, c_t, tq=512)
    out = _mix_ffn(x.reshape(b * s, d), o_g, lse_g, o_b.reshape(b * s, FOX_WIDTH),
                   gates.reshape(b * s, 2 * D_MODEL), w_dil_out[layer], w_fox_out[layer],
                   w_out[layer], norm_ffn_g[layer], w_ffn_in[layer], w_ffn_down[layer],
                   norm_final_g, tm=512)
    return out.reshape(b, s, d)
```

```python
import functools

import jax
import jax.numpy as jnp
from jax import lax
from jax.experimental import pallas as pl
from jax.experimental.pallas import tpu as pltpu

D_MODEL = 1024
HEAD_DIM = 64
DIL_PAIRS = ((128, 1), (512, 4), (2048, 16))
N_DIL_GROUPS = len(DIL_PAIRS)
HEADS_PER_DIL_GROUP = 4
N_DIL_HEADS = N_DIL_GROUPS * HEADS_PER_DIL_GROUP
DIL_OUT_WIDTH = HEADS_PER_DIL_GROUP * HEAD_DIM
DIL_QKV_WIDTH = 3 * DIL_OUT_WIDTH
DIL_WIDTH = N_DIL_HEADS * HEAD_DIM
N_FOX_HEADS = 8
FOX_WIDTH = N_FOX_HEADS * HEAD_DIM
D_FF = 2816
RMS_EPS = 1e-6
NEG_INF = -1e30
ATTN_SCALE = HEAD_DIM ** -0.5
LOG2E = 1.4426950408889634

LANES = 128
DIL_BLOCK = 128
DIL_SPAN = 2048
F_ROWS = 16
VMEM_LIMIT = 56 * 1024 * 1024

F32 = jnp.float32
BF16 = jnp.bfloat16


def _dot(a, b):
    return jnp.dot(a, b, preferred_element_type=F32)


def _dot_nt(a, b):
    return lax.dot_general(a, b, (((1,), (1,)), ((), ())), preferred_element_type=F32)


def _rms_scale(x):
    return lax.rsqrt(jnp.mean(x * x, axis=-1, keepdims=True) + RMS_EPS)


def _rms(x, g):
    return x * _rms_scale(x) * g


def _resident(shape, index_map):
    return pl.BlockSpec(shape, index_map, pipeline_mode=pl.Buffered(1))


def _inproj_kernel(x_ref, g_ref, wa0_ref, wa1_ref, wa2_ref, wfq_ref, wfk_ref, wfv_ref,
                   wf_ref, bf_ref, wg_ref, bg_ref,
                   a0_ref, a1_ref, a2_ref, fq_ref, fk_ref, fv_ref, c_ref, ct_ref, gate_ref,
                   carry_ref, *y_refs):
    tm = x_ref.shape[1]

    @pl.when(pl.program_id(1) == 0)
    def _():
        carry_ref[...] = jnp.zeros_like(carry_ref)

    x = x_ref[0]
    h = (x * g_ref[...]).astype(BF16)
    scale = _rms_scale(x)
    scale_t = jnp.broadcast_to(scale, (tm, LANES)).T[:F_ROWS]

    z = _dot_nt(wf_ref[...], h) * scale_t + bf_ref[...]
    logf = (jnp.minimum(z, 0.0) - jnp.log1p(jnp.exp(-jnp.abs(z)))) * LOG2E
    hi = logf.astype(BF16)
    r1 = logf - hi.astype(F32)
    mid = r1.astype(BF16)
    lo = (r1 - mid.astype(F32)).astype(BF16)
    row = lax.broadcasted_iota(jnp.int32, (tm, tm), 0)
    col = lax.broadcasted_iota(jnp.int32, (tm, tm), 1)
    upto = jnp.where(row <= col, 1.0, 0.0).astype(BF16)
    c3 = _dot(jnp.concatenate([hi, mid, lo], axis=0), upto)
    c_t = c3[:F_ROWS] + c3[F_ROWS:2 * F_ROWS] + c3[2 * F_ROWS:] + carry_ref[...]
    carry_ref[...] = c_t[:, tm - 1:tm]
    ct_ref[0] = c_t[:N_FOX_HEADS]
    c_pad = jnp.concatenate([c_t, jnp.zeros((LANES - F_ROWS, tm), F32)], axis=0)
    c_ref[0] = c_pad.T[:, :N_FOX_HEADS]

    for (_, dilation), w_ref, o_ref, y_ref in zip(DIL_PAIRS, (wa0_ref, wa1_ref, wa2_ref),
                                                  (a0_ref, a1_ref, a2_ref), (None,) + y_refs):
        y = _dot(h, w_ref[...]) * scale
        if dilation == 1:
            o_ref[0] = y.astype(BF16)
        else:
            for cb in range(DIL_QKV_WIDTH // LANES):
                y_ref[cb] = y[:, cb * LANES:(cb + 1) * LANES]
            for r in range(dilation):
                for cb in range(DIL_QKV_WIDTH // LANES):
                    c0 = r * DIL_QKV_WIDTH + cb * LANES
                    o_ref[0, :, c0:c0 + LANES] = (
                        y_ref[cb, pl.ds(r, tm // dilation, stride=dilation), :].astype(BF16))
    for w_ref, o_ref in ((wfq_ref, fq_ref), (wfk_ref, fk_ref), (wfv_ref, fv_ref)):
        o_ref[0] = (_dot(h, w_ref[...]) * scale).astype(BF16)
    gate_ref[0] = jax.nn.sigmoid(_dot(h, wg_ref[...]) * scale + bg_ref[...]).astype(BF16)


def _inproj(x, norm_g, w_in, b_fgt, b_gate, tm):
    b, s, _ = x.shape
    o2 = 3 * DIL_WIDTH
    o3 = o2 + 3 * FOX_WIDTH
    o4 = o3 + N_FOX_HEADS
    wq = w_in[:, :DIL_WIDTH] * (ATTN_SCALE * LOG2E)
    wk = w_in[:, DIL_WIDTH:2 * DIL_WIDTH]
    wv = w_in[:, 2 * DIL_WIDTH:o2]
    w = DIL_OUT_WIDTH
    wa = [jnp.concatenate([wq[:, g * w:(g + 1) * w], wk[:, g * w:(g + 1) * w],
                           wv[:, g * w:(g + 1) * w]], axis=1).astype(BF16)
          for g in range(N_DIL_GROUPS)]
    wfq = (w_in[:, o2:o2 + FOX_WIDTH] * (ATTN_SCALE * LOG2E)).astype(BF16)
    wfk = w_in[:, o2 + FOX_WIDTH:o2 + 2 * FOX_WIDTH].astype(BF16)
    wfv = w_in[:, o2 + 2 * FOX_WIDTH:o3].astype(BF16)
    wf = jnp.pad(w_in[:, o3:o4].T, ((0, F_ROWS - N_FOX_HEADS), (0, 0))).astype(BF16)
    bf = jnp.pad(b_fgt, (0, F_ROWS - N_FOX_HEADS)).reshape(F_ROWS, 1)
    wg = w_in[:, o4:].astype(BF16)
    bg = b_gate.reshape(1, -1)

    def const(shape):
        return _resident(shape, lambda bi, si: (0,) * len(shape))

    def tok(width, rows=tm):
        return pl.BlockSpec((1, rows, width), lambda bi, si: (bi, si, 0))

    dils = [d for _, d in DIL_PAIRS]
    assert all(tm % (16 * d) == 0 for d in dils)
    out_shape = [jax.ShapeDtypeStruct((b, s // d, d * DIL_QKV_WIDTH), BF16) for d in dils]
    out_shape += [jax.ShapeDtypeStruct((b, s, FOX_WIDTH), BF16)] * 3
    out_shape += [jax.ShapeDtypeStruct((b, s, N_FOX_HEADS), F32),
                  jax.ShapeDtypeStruct((b, N_FOX_HEADS, s), F32),
                  jax.ShapeDtypeStruct((b, s, 2 * D_MODEL), BF16)]
    out_specs = [tok(d * DIL_QKV_WIDTH, tm // d) for d in dils] + [tok(FOX_WIDTH)] * 3
    out_specs += [tok(N_FOX_HEADS),
                  pl.BlockSpec((1, N_FOX_HEADS, tm), lambda bi, si: (bi, 0, si)),
                  tok(2 * D_MODEL)]
    in_specs = [tok(D_MODEL), const((1, D_MODEL))]
    in_specs += [const((D_MODEL, DIL_QKV_WIDTH))] * 3 + [const((D_MODEL, FOX_WIDTH))] * 3
    in_specs += [const((F_ROWS, D_MODEL)), const((F_ROWS, 1)),
                 const((D_MODEL, 2 * D_MODEL)), const((1, 2 * D_MODEL))]
    return pl.pallas_call(
        _inproj_kernel,
        grid=(b, s // tm),
        in_specs=in_specs,
        out_specs=out_specs,
        out_shape=out_shape,
        scratch_shapes=[pltpu.VMEM((F_ROWS, 1), F32),
                        pltpu.VMEM((DIL_QKV_WIDTH // LANES, tm, LANES), F32),
                        pltpu.VMEM((DIL_QKV_WIDTH // LANES, tm, LANES), F32)],
        compiler_params=pltpu.CompilerParams(
            dimension_semantics=("arbitrary", "arbitrary"), vmem_limit_bytes=VMEM_LIMIT),
        name="inproj",
    )(x, norm_g.reshape(1, -1), *wa, wfq, wfk, wfv, wf, bf, wg, bg)


def _dil_kernel(cur_ref, prev_ref, o_ref, lse_ref, *, dilation, slopes, nsub):
    w = DIL_OUT_WIDTH
    blk = DIL_BLOCK
    n = pl.program_id(1)
    row = lax.broadcasted_iota(jnp.int32, (blk, 2 * blk), 0)
    col = lax.broadcasted_iota(jnp.int32, (blk, 2 * blk), 1)
    rel = row - col + blk
    valid = jnp.logical_and(rel >= 0, rel <= blk)
    valid_first = jnp.logical_and(valid, jnp.logical_or(col >= blk, n > 0))
    dist = (rel * dilation).astype(F32)
    penalty = [[jnp.where(ok, (slope * LOG2E) * dist, -NEG_INF) for slope in slopes]
               for ok in (valid_first, valid)]
    lane = lax.broadcasted_iota(jnp.int32, (blk, LANES), 1)
    low = lane < HEAD_DIM

    for res in range(dilation):
        c0 = res * 3 * w
        for sb in range(nsub):
            r0 = sb * blk
            for hp in range(HEADS_PER_DIL_GROUP // 2):
                lo_, hi_ = c0 + hp * LANES, c0 + (hp + 1) * LANES
                q_pair = cur_ref[0, r0:r0 + blk, lo_:hi_]
                q2 = jnp.concatenate([jnp.where(low, q_pair, jnp.zeros_like(q_pair)),
                                      jnp.where(low, jnp.zeros_like(q_pair), q_pair)], axis=0)
                if sb == 0:
                    k2 = jnp.concatenate([prev_ref[0, :, w + lo_:w + hi_],
                                          cur_ref[0, :blk, w + lo_:w + hi_]], axis=0)
                    v2 = jnp.concatenate([prev_ref[0, :, 2 * w + lo_:2 * w + hi_],
                                          cur_ref[0, :blk, 2 * w + lo_:2 * w + hi_]], axis=0)
                else:
                    k2 = cur_ref[0, r0 - blk:r0 + blk, w + lo_:w + hi_]
                    v2 = cur_ref[0, r0 - blk:r0 + blk, 2 * w + lo_:2 * w + hi_]
                s2 = _dot_nt(q2, k2)
                ps, ms = [], []
                for j in range(2):
                    s = s2[j * blk:(j + 1) * blk] - penalty[min(sb, 1)][2 * hp + j]
                    m = jnp.max(s, axis=-1, keepdims=True)
                    ps.append(jnp.exp2(s - m).astype(BF16))
                    ms.append(jnp.broadcast_to(m, (blk, LANES)))
                acc2 = _dot(jnp.concatenate(ps, axis=0),
                            jnp.concatenate([v2, jnp.ones_like(v2)], axis=1))
                den = acc2[:, LANES:]
                out = acc2[:, :LANES] / den
                lse = jnp.concatenate(ms, axis=0) + jnp.log2(den)
                tokens = pl.ds(r0 * dilation + res, blk, stride=dilation)
                o_ref[0, hp, tokens, :] = jnp.where(low, out[:blk], out[blk:])
                lse_ref[0, hp, tokens, :] = jnp.where(low, lse[:blk], lse[blk:])


def _dilated(view, group):
    window, dilation = DIL_PAIRS[group]
    assert window // dilation == DIL_BLOCK
    w = DIL_OUT_WIDTH
    batch, rows, _ = view.shape
    qrows = min(rows, max(DIL_SPAN // dilation, 2 * DIL_BLOCK))
    nsub = qrows // DIL_BLOCK
    span = qrows * dilation
    slopes = tuple(float(2.0 ** (-8.0 * (group * HEADS_PER_DIL_GROUP + h + 1) / N_DIL_HEADS))
                   for h in range(HEADS_PER_DIL_GROUP))
    planes = jax.ShapeDtypeStruct((batch, w // LANES, rows * dilation, LANES), F32)
    plane_spec = pl.BlockSpec((1, w // LANES, span, LANES), lambda b, n: (b, 0, n, 0))
    return pl.pallas_call(
        functools.partial(_dil_kernel, dilation=dilation, slopes=slopes, nsub=nsub),
        grid=(batch, rows // qrows),
        in_specs=[pl.BlockSpec((1, qrows, dilation * 3 * w), lambda b, n: (b, n, 0)),
                  pl.BlockSpec((1, DIL_BLOCK, dilation * 3 * w),
                               lambda b, n: (b, jnp.maximum(n * nsub - 1, 0), 0))],
        out_specs=[plane_spec, plane_spec],
        out_shape=[planes, planes],
        compiler_params=pltpu.CompilerParams(
            dimension_semantics=("arbitrary", "arbitrary"), vmem_limit_bytes=VMEM_LIMIT),
        name=f"dilated{group}",
    )(view, view)


def _fox_kernel(q_ref, k_ref, v_ref, c_ref, ct_ref, o_ref,
                sa_ref, mxa_ref, sb_ref, mxb_ref, sc_ref, mxc_ref,
                m0_ref, acc0_ref, cq0_ref, m1_ref, acc1_ref, cq1_ref, v1_ref, *, tq, n_steps):
    hp = pl.program_id(1)
    j = pl.program_id(2)
    n_qtiles = k_ref.shape[1] // tq
    lane = lax.broadcasted_iota(jnp.int32, (tq, LANES), 1)
    low = lane < HEAD_DIM
    head_lane = lax.broadcasted_iota(jnp.int32, (tq, N_FOX_HEADS), 1)
    row = lax.broadcasted_iota(jnp.int32, (tq, tq), 0)
    col = lax.broadcasted_iota(jnp.int32, (tq, tq), 1)
    causal = col <= row

    @pl.when(j == 0)
    def _():
        v1_ref[:, :LANES] = v_ref[0]
        v1_ref[:, LANES:] = jnp.ones((v1_ref.shape[0], LANES), BF16)

    bufs = ((sa_ref, mxa_ref), (sb_ref, mxb_ref), (sc_ref, mxc_ref))
    states = ((m0_ref, acc0_ref, cq0_ref), (m1_ref, acc1_ref, cq1_ref))

    class QTile:
        def __init__(self, qi, state):
            self.qi = qi
            self.m, self.acc, self.cq = state
            self.rows = slice(qi * tq, (qi + 1) * tq)
            q_pair = q_ref[0, self.rows, :]
            self.q2 = jnp.concatenate([jnp.where(low, q_pair, jnp.zeros_like(q_pair)),
                                       jnp.where(low, jnp.zeros_like(q_pair), q_pair)], axis=0)
            c_blk = c_ref[0, self.rows, :]
            for h in range(2):
                self.cq[h * tq:(h + 1) * tq, :] = jnp.broadcast_to(
                    jnp.sum(jnp.where(head_lane == 2 * hp + h, c_blk, 0.0),
                            axis=-1, keepdims=True), (tq, LANES))
            self.m[...] = jnp.full_like(self.m, NEG_INF)
            self.acc[...] = jnp.zeros_like(self.acc)

        def produce(self, kt, buf):
            s_ref, mx_ref = buf
            keys = slice(kt * tq, (kt + 1) * tq)
            s2 = _dot_nt(self.q2, k_ref[0, keys, :])
            for h in range(2):
                rows = slice(h * tq, (h + 1) * tq)
                s = s2[rows] - ct_ref[0, pl.ds(2 * hp + h, 1), keys]
                if kt == self.qi:
                    s = jnp.where(causal, s, NEG_INF)
                s_ref[rows, :] = s
                mx_ref[rows, :] = jnp.broadcast_to(
                    jnp.max(s, axis=-1, keepdims=True), (tq, LANES)) + self.cq[rows, :]

        def consume(self, kt, buf):
            s_ref, mx_ref = buf
            m = self.m[...]
            m_new = jnp.maximum(m, mx_ref[...])
            alpha = jnp.exp2(m - m_new)
            p = jnp.exp2(s_ref[...] - jnp.tile(m_new - self.cq[...], (1, tq // LANES)))
            self.acc[...] = (jnp.tile(alpha, (1, 2)) * self.acc[...]
                             + _dot(p.astype(BF16), v1_ref[kt * tq:(kt + 1) * tq, :]))
            self.m[...] = m_new

        def finish(self):
            out = self.acc[:, :LANES] / self.acc[:, LANES:]
            o_ref[0, self.rows, :] = jnp.where(low, out[:tq], out[tq:]).astype(o_ref.dtype)

    def schedule(qtiles):
        work = [(qi, kt) for qi in qtiles for kt in [qi] + list(range(qi))]
        live = {}

        def tile(qi):
            if qi not in live:
                live[qi] = QTile(qi, states[qtiles.index(qi) % len(states)])
            return live[qi]

        ahead = len(bufs) - 1
        for n in range(min(ahead, len(work))):
            qi, kt = work[n]
            tile(qi).produce(kt, bufs[n])
        for n, (qi, kt) in enumerate(work):
            tile(qi).consume(kt, bufs[n % len(bufs)])
            if n + 1 == len(work) or work[n + 1][0] != qi:
                tile(qi).finish()
            if n + ahead < len(work):
                qi2, kt2 = work[n + ahead]
                tile(qi2).produce(kt2, bufs[(n + ahead) % len(bufs)])

    pairs = [(i, n_qtiles - 1 - i) for i in range(n_qtiles // 2)]
    assert n_qtiles % 2 == 0 and len(pairs) % n_steps == 0
    per_step = len(pairs) // n_steps
    for step in range(n_steps):
        qtiles = [qi for pair in pairs[step * per_step:(step + 1) * per_step] for qi in pair]
        pl.when(j == step)(functools.partial(schedule, qtiles))


def _fox(fq, fk, fv, c, c_t, tq, n_steps):
    b, s, _ = fq.shape
    rows = 2 * tq

    def whole(width):
        return pl.BlockSpec((1, s, width), lambda bi, hp, j: (bi, 0, hp))

    return pl.pallas_call(
        functools.partial(_fox_kernel, tq=tq, n_steps=n_steps),
        grid=(b, N_FOX_HEADS // 2, n_steps),
        in_specs=[whole(LANES), whole(LANES), whole(LANES),
                  pl.BlockSpec((1, s, N_FOX_HEADS), lambda bi, hp, j: (bi, 0, 0)),
                  pl.BlockSpec((1, N_FOX_HEADS, s), lambda bi, hp, j: (bi, 0, 0))],
        out_specs=whole(LANES),
        out_shape=jax.ShapeDtypeStruct((b, s, FOX_WIDTH), BF16),
        scratch_shapes=[pltpu.VMEM((rows, tq), F32),
                        pltpu.VMEM((rows, LANES), F32)] * 3
        + [pltpu.VMEM((rows, LANES), F32),
           pltpu.VMEM((rows, 2 * LANES), F32),
           pltpu.VMEM((rows, LANES), F32)] * 2
        + [pltpu.VMEM((s, 2 * LANES), BF16)],
        compiler_params=pltpu.CompilerParams(
            dimension_semantics=("arbitrary", "arbitrary", "arbitrary"),
            vmem_limit_bytes=VMEM_LIMIT),
        name="fox",
    )(fq, fk, fv, c, c_t)


FF_CHUNKS = ((0, 512), (512, 1024), (1024, 1536), (1536, 2048), (2048, 2560), (2560, D_FF))


def _mix_ffn_kernel(x_ref, o0_ref, o1_ref, o2_ref, l0_ref, l1_ref, l2_ref, ob_ref, gate_ref,
                    wd_ref, wfo_ref, wo_ref, g2_ref, wgi_ref, wui_ref, wdn_ref, gfin_ref,
                    out_ref):
    def planes(ref):
        return jnp.concatenate([ref[0, p] for p in range(ref.shape[1])], axis=1)

    o0, o1, o2 = planes(o0_ref), planes(o1_ref), planes(o2_ref)
    l0, l1, l2 = planes(l0_ref), planes(l1_ref), planes(l2_ref)
    m = jnp.maximum(jnp.maximum(l0, l1), l2)
    e0, e1, e2 = jnp.exp2(l0 - m), jnp.exp2(l1 - m), jnp.exp2(l2 - m)
    o_a = (e0 * o0 + e1 * o1 + e2 * o2) / (e0 + e1 + e2)
    y_a = _dot(o_a.astype(BF16), wd_ref[...])
    y_b = _dot(ob_ref[...], wfo_ref[...])
    merged = (gate_ref[:, :D_MODEL].astype(F32) * y_a
              + gate_ref[:, D_MODEL:].astype(F32) * y_b)
    x1 = x_ref[...] + _dot(merged.astype(BF16), wo_ref[...])

    h2 = (x1 * g2_ref[...]).astype(BF16)
    scale = _rms_scale(x1)
    acc = jnp.zeros_like(x1)
    for c0, c1 in FF_CHUNKS:
        gte = _dot(h2, wgi_ref[:, c0:c1]) * scale
        up = _dot(h2, wui_ref[:, c0:c1]) * scale
        act = gte * jax.nn.sigmoid(gte) * up
        acc = acc + _dot(act.astype(BF16), wdn_ref[c0:c1, :])
    out_ref[...] = _rms(x1 + acc, gfin_ref[...])


def _mix_ffn(x2d, o_g, lse_g, o_b, gates, w_dil_out, w_fox_out, w_out, norm_ffn_g,
             w_ffn_in, w_ffn_down, norm_final_g, tm):
    n = x2d.shape[0]
    w = DIL_OUT_WIDTH
    batch, n_planes, seq, _ = o_g[0].shape
    tiles = seq // tm

    def const(shape):
        return _resident(shape, lambda i: (0,) * len(shape))

    def tok(width):
        return pl.BlockSpec((tm, width), lambda i: (i, 0))

    dil_specs = [pl.BlockSpec((1, n_planes, tm, LANES),
                              lambda i: (i // tiles, 0, i % tiles, 0))] * N_DIL_GROUPS
    in_specs = [tok(D_MODEL)] + dil_specs + dil_specs + [tok(FOX_WIDTH), tok(2 * D_MODEL)]
    in_specs += [const((w, D_MODEL)), const((FOX_WIDTH, D_MODEL)),
                 const((D_MODEL, D_MODEL)), const((1, D_MODEL)),
                 const((D_MODEL, D_FF)), const((D_MODEL, D_FF)), const((D_FF, D_MODEL)),
                 const((1, D_MODEL))]
    return pl.pallas_call(
        _mix_ffn_kernel,
        grid=(n // tm,),
        in_specs=in_specs,
        out_specs=tok(D_MODEL),
        out_shape=jax.ShapeDtypeStruct((n, D_MODEL), F32),
        compiler_params=pltpu.CompilerParams(
            dimension_semantics=("arbitrary",), vmem_limit_bytes=VMEM_LIMIT),
        name="mix_ffn",
    )(x2d, *o_g, *lse_g, o_b, gates,
      w_dil_out.astype(BF16), w_fox_out.astype(BF16), w_out.astype(BF16),
      norm_ffn_g.reshape(1, -1),
      w_ffn_in[:, :D_FF].astype(BF16), w_ffn_in[:, D_FF:].astype(BF16),
      w_ffn_down.astype(BF16), norm_final_g.reshape(1, -1))


def kernel(x, norm_mix_g, w_in, b_fgt, b_gate, w_dil_out, w_fox_out, w_out, norm_ffn_g,
           w_ffn_in, w_ffn_down, norm_final_g):
    b, s, d = x.shape
    assert w_in.shape[0] == 1
    layer = 0
    a0, a1, a2, fq, fk, fv, c, c_t, gates = _inproj(
        x, norm_mix_g[layer], w_in[layer], b_fgt[layer], b_gate[layer], tm=512)
    o_g, lse_g = zip(*[_dilated(a, g) for g, a in enumerate((a0, a1, a2))])
    o_b = _fox(fq, fk, fv, c, c_t, tq=512, n_steps=2)
    out = _mix_ffn(x.reshape(b * s, d), o_g, lse_g, o_b.reshape(b * s, FOX_WIDTH),
                   gates.reshape(b * s, 2 * D_MODEL), w_dil_out[layer], w_fox_out[layer],
                   w_out[layer], norm_ffn_g[layer], w_ffn_in[layer], w_ffn_down[layer],
                   norm_final_g, tm=512)
    return out.reshape(b, s, d)
```

```python
import functools

import jax
import jax.numpy as jnp
from jax import lax
from jax.experimental import pallas as pl
from jax.experimental.pallas import tpu as pltpu

D_MODEL = 1024
HEAD_DIM = 64
DIL_PAIRS = ((128, 1), (512, 4), (2048, 16))
N_DIL_GROUPS = len(DIL_PAIRS)
HEADS_PER_DIL_GROUP = 4
N_DIL_HEADS = N_DIL_GROUPS * HEADS_PER_DIL_GROUP
DIL_OUT_WIDTH = HEADS_PER_DIL_GROUP * HEAD_DIM
DIL_QKV_WIDTH = 3 * DIL_OUT_WIDTH
DIL_WIDTH = N_DIL_HEADS * HEAD_DIM
N_FOX_HEADS = 8
FOX_WIDTH = N_FOX_HEADS * HEAD_DIM
D_FF = 2816
RMS_EPS = 1e-6
NEG_INF = -1e30
ATTN_SCALE = HEAD_DIM ** -0.5
LOG2E = 1.4426950408889634

LANES = 128
DIL_BLOCK = 128
DIL_SPAN = 2048
F_ROWS = 16
VMEM_LIMIT = 56 * 1024 * 1024

F32 = jnp.float32
BF16 = jnp.bfloat16


def _dot(a, b):
    return jnp.dot(a, b, preferred_element_type=F32)


def _dot_nt(a, b):
    return lax.dot_general(a, b, (((1,), (1,)), ((), ())), preferred_element_type=F32)


def _rms_scale(x):
    return lax.rsqrt(jnp.mean(x * x, axis=-1, keepdims=True) + RMS_EPS)


def _rms(x, g):
    return x * _rms_scale(x) * g


def _resident(shape, index_map):
    return pl.BlockSpec(shape, index_map, pipeline_mode=pl.Buffered(1))


def _inproj_kernel(x_ref, g_ref, wa0_ref, wa1_ref, wa2_ref, wfq_ref, wfk_ref, wfv_ref,
                   wf_ref, bf_ref, wg_ref, bg_ref,
                   a0_ref, a1_ref, a2_ref, fq_ref, fk_ref, fv_ref, c_ref, ct_ref, gate_ref,
                   carry_ref, *y_refs):
    tm = x_ref.shape[1]

    @pl.when(pl.program_id(1) == 0)
    def _():
        carry_ref[...] = jnp.zeros_like(carry_ref)

    x = x_ref[0]
    h = (x * g_ref[...]).astype(BF16)
    scale = _rms_scale(x)
    scale_t = jnp.broadcast_to(scale, (tm, LANES)).T[:F_ROWS]

    z = _dot_nt(wf_ref[...], h) * scale_t + bf_ref[...]
    logf = (jnp.minimum(z, 0.0) - jnp.log1p(jnp.exp(-jnp.abs(z)))) * LOG2E
    hi = logf.astype(BF16)
    r1 = logf - hi.astype(F32)
    mid = r1.astype(BF16)
    lo = (r1 - mid.astype(F32)).astype(BF16)
    row = lax.broadcasted_iota(jnp.int32, (tm, tm), 0)
    col = lax.broadcasted_iota(jnp.int32, (tm, tm), 1)
    upto = jnp.where(row <= col, 1.0, 0.0).astype(BF16)
    c3 = _dot(jnp.concatenate([hi, mid, lo], axis=0), upto)
    c_t = c3[:F_ROWS] + c3[F_ROWS:2 * F_ROWS] + c3[2 * F_ROWS:] + carry_ref[...]
    carry_ref[...] = c_t[:, tm - 1:tm]
    ct_ref[0] = c_t[:N_FOX_HEADS]
    c_pad = jnp.concatenate([c_t, jnp.zeros((LANES - F_ROWS, tm), F32)], axis=0)
    c_ref[0] = c_pad.T[:, :N_FOX_HEADS]

    for (_, dilation), w_ref, o_ref, y_ref in zip(DIL_PAIRS, (wa0_ref, wa1_ref, wa2_ref),
                                                  (a0_ref, a1_ref, a2_ref), (None,) + y_refs):
        y = _dot(h, w_ref[...]) * scale
        if dilation == 1:
            o_ref[0] = y.astype(BF16)
        else:
            for cb in range(DIL_QKV_WIDTH // LANES):
                y_ref[cb] = y[:, cb * LANES:(cb + 1) * LANES]
            for r in range(dilation):
                for cb in range(DIL_QKV_WIDTH // LANES):
                    c0 = r * DIL_QKV_WIDTH + cb * LANES
                    o_ref[0, :, c0:c0 + LANES] = (
                        y_ref[cb, pl.ds(r, tm // dilation, stride=dilation), :].astype(BF16))
    for w_ref, o_ref in ((wfq_ref, fq_ref), (wfk_ref, fk_ref), (wfv_ref, fv_ref)):
        o_ref[0] = (_dot(h, w_ref[...]) * scale).astype(BF16)
    gate_ref[0] = jax.nn.sigmoid(_dot(h, wg_ref[...]) * scale + bg_ref[...]).astype(BF16)


def _inproj(x, norm_g, w_in, b_fgt, b_gate, tm):
    b, s, _ = x.shape
    o2 = 3 * DIL_WIDTH
    o3 = o2 + 3 * FOX_WIDTH
    o4 = o3 + N_FOX_HEADS
    wq = w_in[:, :DIL_WIDTH] * (ATTN_SCALE * LOG2E)
    wk = w_in[:, DIL_WIDTH:2 * DIL_WIDTH]
    wv = w_in[:, 2 * DIL_WIDTH:o2]
    w = DIL_OUT_WIDTH
    wa = [jnp.concatenate([wq[:, g * w:(g + 1) * w], wk[:, g * w:(g + 1) * w],
                           wv[:, g * w:(g + 1) * w]], axis=1).astype(BF16)
          for g in range(N_DIL_GROUPS)]
    wfq = (w_in[:, o2:o2 + FOX_WIDTH] * (ATTN_SCALE * LOG2E)).astype(BF16)
    wfk = w_in[:, o2 + FOX_WIDTH:o2 + 2 * FOX_WIDTH].astype(BF16)
    wfv = w_in[:, o2 + 2 * FOX_WIDTH:o3].astype(BF16)
    wf = jnp.pad(w_in[:, o3:o4].T, ((0, F_ROWS - N_FOX_HEADS), (0, 0))).astype(BF16)
    bf = jnp.pad(b_fgt, (0, F_ROWS - N_FOX_HEADS)).reshape(F_ROWS, 1)
    wg = w_in[:, o4:].astype(BF16)
    bg = b_gate.reshape(1, -1)

    def const(shape):
        return _resident(shape, lambda bi, si: (0,) * len(shape))

    def tok(width, rows=tm):
        return pl.BlockSpec((1, rows, width), lambda bi, si: (bi, si, 0))

    dils = [d for _, d in DIL_PAIRS]
    assert all(tm % (16 * d) == 0 for d in dils)
    out_shape = [jax.ShapeDtypeStruct((b, s // d, d * DIL_QKV_WIDTH), BF16) for d in dils]
    out_shape += [jax.ShapeDtypeStruct((b, s, FOX_WIDTH), BF16)] * 3
    out_shape += [jax.ShapeDtypeStruct((b, s, N_FOX_HEADS), F32),
                  jax.ShapeDtypeStruct((b, N_FOX_HEADS, s), F32),
                  jax.ShapeDtypeStruct((b, s, 2 * D_MODEL), BF16)]
    out_specs = [tok(d * DIL_QKV_WIDTH, tm // d) for d in dils] + [tok(FOX_WIDTH)] * 3
    out_specs += [tok(N_FOX_HEADS),
                  pl.BlockSpec((1, N_FOX_HEADS, tm), lambda bi, si: (bi, 0, si)),
                  tok(2 * D_MODEL)]
    in_specs = [tok(D_MODEL), const((1, D_MODEL))]
    in_specs += [const((D_MODEL, DIL_QKV_WIDTH))] * 3 + [const((D_MODEL, FOX_WIDTH))] * 3
    in_specs += [const((F_ROWS, D_MODEL)), const((F_ROWS, 1)),
                 const((D_MODEL, 2 * D_MODEL)), const((1, 2 * D_MODEL))]
    return pl.pallas_call(
        _inproj_kernel,
        grid=(b, s // tm),
        in_specs=in_specs,
        out_specs=out_specs,
        out_shape=out_shape,
        scratch_shapes=[pltpu.VMEM((F_ROWS, 1), F32),
                        pltpu.VMEM((DIL_QKV_WIDTH // LANES, tm, LANES), F32),
                        pltpu.VMEM((DIL_QKV_WIDTH // LANES, tm, LANES), F32)],
        compiler_params=pltpu.CompilerParams(
            dimension_semantics=("arbitrary", "arbitrary"), vmem_limit_bytes=VMEM_LIMIT),
        name="inproj",
    )(x, norm_g.reshape(1, -1), *wa, wfq, wfk, wfv, wf, bf, wg, bg)


def _dil_kernel(cur_ref, prev_ref, o_ref, lse_ref, *, dilation, slopes, nsub):
    w = DIL_OUT_WIDTH
    blk = DIL_BLOCK
    n = pl.program_id(1)
    row = lax.broadcasted_iota(jnp.int32, (blk, 2 * blk), 0)
    col = lax.broadcasted_iota(jnp.int32, (blk, 2 * blk), 1)
    rel = row - col + blk
    valid = jnp.logical_and(rel >= 0, rel <= blk)
    valid_first = jnp.logical_and(valid, jnp.logical_or(col >= blk, n > 0))
    dist = (rel * dilation).astype(F32)
    penalty = [[jnp.where(ok, (slope * LOG2E) * dist, -NEG_INF) for slope in slopes]
               for ok in (valid_first, valid)]
    lane = lax.broadcasted_iota(jnp.int32, (blk, LANES), 1)
    low = lane < HEAD_DIM

    for res in range(dilation):
        c0 = res * 3 * w
        for sb in range(nsub):
            r0 = sb * blk
            for hp in range(HEADS_PER_DIL_GROUP // 2):
                lo_, hi_ = c0 + hp * LANES, c0 + (hp + 1) * LANES
                q_pair = cur_ref[0, r0:r0 + blk, lo_:hi_]
                q2 = jnp.concatenate([jnp.where(low, q_pair, jnp.zeros_like(q_pair)),
                                      jnp.where(low, jnp.zeros_like(q_pair), q_pair)], axis=0)
                if sb == 0:
                    k2 = jnp.concatenate([prev_ref[0, :, w + lo_:w + hi_],
                                          cur_ref[0, :blk, w + lo_:w + hi_]], axis=0)
                    v2 = jnp.concatenate([prev_ref[0, :, 2 * w + lo_:2 * w + hi_],
                                          cur_ref[0, :blk, 2 * w + lo_:2 * w + hi_]], axis=0)
                else:
                    k2 = cur_ref[0, r0 - blk:r0 + blk, w + lo_:w + hi_]
                    v2 = cur_ref[0, r0 - blk:r0 + blk, 2 * w + lo_:2 * w + hi_]
                s2 = _dot_nt(q2, k2)
                ps, ms = [], []
                for j in range(2):
                    s = s2[j * blk:(j + 1) * blk] - penalty[min(sb, 1)][2 * hp + j]
                    m = jnp.max(s, axis=-1, keepdims=True)
                    ps.append(jnp.exp2(s - m).astype(BF16))
                    ms.append(jnp.broadcast_to(m, (blk, LANES)))
                acc2 = _dot(jnp.concatenate(ps, axis=0),
                            jnp.concatenate([v2, jnp.ones_like(v2)], axis=1))
                den = acc2[:, LANES:]
                out = acc2[:, :LANES] / den
                lse = jnp.concatenate(ms, axis=0) + jnp.log2(den)
                tokens = pl.ds(r0 * dilation + res, blk, stride=dilation)
                o_ref[0, hp, tokens, :] = jnp.where(low, out[:blk], out[blk:])
                lse_ref[0, hp, tokens, :] = jnp.where(low, lse[:blk], lse[blk:])


def _dilated(view, group):
    window, dilation = DIL_PAIRS[group]
    assert window // dilation == DIL_BLOCK
    w = DIL_OUT_WIDTH
    batch, rows, _ = view.shape
    qrows = min(rows, max(DIL_SPAN // dilation, 2 * DIL_BLOCK))
    nsub = qrows // DIL_BLOCK
    span = qrows * dilation
    slopes = tuple(float(2.0 ** (-8.0 * (group * HEADS_PER_DIL_GROUP + h + 1) / N_DIL_HEADS))
                   for h in range(HEADS_PER_DIL_GROUP))
    planes = jax.ShapeDtypeStruct((batch, w // LANES, rows * dilation, LANES), F32)
    plane_spec = pl.BlockSpec((1, w // LANES, span, LANES), lambda b, n: (b, 0, n, 0))
    return pl.pallas_call(
        functools.partial(_dil_kernel, dilation=dilation, slopes=slopes, nsub=nsub),
        grid=(batch, rows // qrows),
        in_specs=[pl.BlockSpec((1, qrows, dilation * 3 * w), lambda b, n: (b, n, 0)),
                  pl.BlockSpec((1, DIL_BLOCK, dilation * 3 * w),
                               lambda b, n: (b, jnp.maximum(n * nsub - 1, 0), 0))],
        out_specs=[plane_spec, plane_spec],
        out_shape=[planes, planes],
        compiler_params=pltpu.CompilerParams(
            dimension_semantics=("arbitrary", "arbitrary"), vmem_limit_bytes=VMEM_LIMIT),
        name=f"dilated{group}",
    )(view, view)


def _fox_kernel(q_ref, k_ref, v_ref, c_ref, ct_ref, o_ref,
                sa_ref, mxa_ref, sb_ref, mxb_ref, sc_ref, mxc_ref, sd_ref, mxd_ref,
                m0_ref, acc0_ref, cq0_ref, m1_ref, acc1_ref, cq1_ref, v1_ref, *, tq):
    hp = pl.program_id(1)
    n_qtiles = k_ref.shape[1] // tq
    lane = lax.broadcasted_iota(jnp.int32, (tq, LANES), 1)
    low = lane < HEAD_DIM
    head_lane = lax.broadcasted_iota(jnp.int32, (tq, N_FOX_HEADS), 1)
    row = lax.broadcasted_iota(jnp.int32, (tq, tq), 0)
    col = lax.broadcasted_iota(jnp.int32, (tq, tq), 1)
    causal = col <= row

    v1_ref[:, :LANES] = v_ref[0]
    v1_ref[:, LANES:] = jnp.ones((v1_ref.shape[0], LANES), BF16)

    bufs = ((sa_ref, mxa_ref), (sb_ref, mxb_ref), (sc_ref, mxc_ref), (sd_ref, mxd_ref))
    states = ((m0_ref, acc0_ref, cq0_ref), (m1_ref, acc1_ref, cq1_ref))

    class QTile:
        def __init__(self, qi, state):
            self.qi = qi
            self.m, self.acc, self.cq = state
            self.rows = slice(qi * tq, (qi + 1) * tq)
            q_pair = q_ref[0, self.rows, :]
            self.q2 = jnp.concatenate([jnp.where(low, q_pair, jnp.zeros_like(q_pair)),
                                       jnp.where(low, jnp.zeros_like(q_pair), q_pair)], axis=0)
            c_blk = c_ref[0, self.rows, :]
            for h in range(2):
                self.cq[h * tq:(h + 1) * tq, :] = jnp.broadcast_to(
                    jnp.sum(jnp.where(head_lane == 2 * hp + h, c_blk, 0.0),
                            axis=-1, keepdims=True), (tq, LANES))
            self.m[...] = jnp.full_like(self.m, NEG_INF)
            self.acc[...] = jnp.zeros_like(self.acc)

        def produce(self, kt, buf):
            s_ref, mx_ref = buf
            keys = slice(kt * tq, (kt + 1) * tq)
            s2 = _dot_nt(self.q2, k_ref[0, keys, :])
            for h in range(2):
                rows = slice(h * tq, (h + 1) * tq)
                s = s2[rows] - ct_ref[0, pl.ds(2 * hp + h, 1), keys]
                if kt == self.qi:
                    s = jnp.where(causal, s, NEG_INF)
                s_ref[rows, :] = s
                mx_ref[rows, :] = jnp.broadcast_to(
                    jnp.max(s, axis=-1, keepdims=True), (tq, LANES)) + self.cq[rows, :]

        def consume(self, kt, buf):
            s_ref, mx_ref = buf
            m = self.m[...]
            m_new = jnp.maximum(m, mx_ref[...])
            alpha = jnp.exp2(m - m_new)
            p = jnp.exp2(s_ref[...] - jnp.tile(m_new - self.cq[...], (1, tq // LANES)))
            self.acc[...] = (jnp.tile(alpha, (1, 2)) * self.acc[...]
                             + _dot(p.astype(BF16), v1_ref[kt * tq:(kt + 1) * tq, :]))
            self.m[...] = m_new

        def finish(self):
            out = self.acc[:, :LANES] / self.acc[:, LANES:]
            o_ref[0, self.rows, :] = jnp.where(low, out[:tq], out[tq:]).astype(o_ref.dtype)

    qtiles = [qi for i in range(n_qtiles // 2) for qi in (i, n_qtiles - 1 - i)]
    if n_qtiles % 2:
        qtiles.append(n_qtiles // 2)
    work = [(qi, kt) for qi in qtiles for kt in [qi] + list(range(qi))]
    live = {}

    def tile(qi):
        if qi not in live:
            live[qi] = QTile(qi, states[qtiles.index(qi) % len(states)])
        return live[qi]

    ahead = len(bufs) - 1
    for n in range(min(ahead, len(work))):
        qi, kt = work[n]
        tile(qi).produce(kt, bufs[n])
    for n, (qi, kt) in enumerate(work):
        tile(qi).consume(kt, bufs[n % len(bufs)])
        if n + 1 == len(work) or work[n + 1][0] != qi:
            tile(qi).finish()
        if n + ahead < len(work):
            qi2, kt2 = work[n + ahead]
            tile(qi2).produce(kt2, bufs[(n + ahead) % len(bufs)])


def _fox(fq, fk, fv, c, c_t, tq):
    b, s, _ = fq.shape
    rows = 2 * tq

    def whole(width):
        return pl.BlockSpec((1, s, width), lambda bi, hp: (bi, 0, hp))

    return pl.pallas_call(
        functools.partial(_fox_kernel, tq=tq),
        grid=(b, N_FOX_HEADS // 2),
        in_specs=[whole(LANES), whole(LANES), whole(LANES),
                  pl.BlockSpec((1, s, N_FOX_HEADS), lambda bi, hp: (bi, 0, 0)),
                  pl.BlockSpec((1, N_FOX_HEADS, s), lambda bi, hp: (bi, 0, 0))],
        out_specs=whole(LANES),
        out_shape=jax.ShapeDtypeStruct((b, s, FOX_WIDTH), BF16),
        scratch_shapes=[pltpu.VMEM((rows, tq), F32),
                        pltpu.VMEM((rows, LANES), F32)] * 4
        + [pltpu.VMEM((rows, LANES), F32),
           pltpu.VMEM((rows, 2 * LANES), F32),
           pltpu.VMEM((rows, LANES), F32)] * 2
        + [pltpu.VMEM((s, 2 * LANES), BF16)],
        compiler_params=pltpu.CompilerParams(
            dimension_semantics=("arbitrary", "arbitrary"), vmem_limit_bytes=VMEM_LIMIT),
        name="fox",
    )(fq, fk, fv, c, c_t)


FF_CHUNKS = ((0, 1024), (1024, 2048), (2048, D_FF))


def _mix_ffn_kernel(x_ref, o0_ref, o1_ref, o2_ref, l0_ref, l1_ref, l2_ref, ob_ref, gate_ref,
                    wd_ref, wfo_ref, wo_ref, g2_ref, wgi_ref, wui_ref, wdn_ref, gfin_ref,
                    out_ref):
    def planes(ref):
        return jnp.concatenate([ref[0, p] for p in range(ref.shape[1])], axis=1)

    o0, o1, o2 = planes(o0_ref), planes(o1_ref), planes(o2_ref)
    l0, l1, l2 = planes(l0_ref), planes(l1_ref), planes(l2_ref)
    m = jnp.maximum(jnp.maximum(l0, l1), l2)
    e0, e1, e2 = jnp.exp2(l0 - m), jnp.exp2(l1 - m), jnp.exp2(l2 - m)
    o_a = (e0 * o0 + e1 * o1 + e2 * o2) / (e0 + e1 + e2)
    y_a = _dot(o_a.astype(BF16), wd_ref[...])
    y_b = _dot(ob_ref[...], wfo_ref[...])
    merged = (gate_ref[:, :D_MODEL].astype(F32) * y_a
              + gate_ref[:, D_MODEL:].astype(F32) * y_b)
    x1 = x_ref[...] + _dot(merged.astype(BF16), wo_ref[...])

    h2 = (x1 * g2_ref[...]).astype(BF16)
    scale = _rms_scale(x1)
    acc = jnp.zeros_like(x1)
    for c0, c1 in FF_CHUNKS:
        gte = _dot(h2, wgi_ref[:, c0:c1]) * scale
        up = _dot(h2, wui_ref[:, c0:c1]) * scale
        act = gte * jax.nn.sigmoid(gte) * up
        acc = acc + _dot(act.astype(BF16), wdn_ref[c0:c1, :])
    out_ref[...] = _rms(x1 + acc, gfin_ref[...])


def _mix_ffn(x2d, o_g, lse_g, o_b, gates, w_dil_out, w_fox_out, w_out, norm_ffn_g,
             w_ffn_in, w_ffn_down, norm_final_g, tm):
    n = x2d.shape[0]
    w = DIL_OUT_WIDTH
    batch, n_planes, seq, _ = o_g[0].shape
    tiles = seq // tm

    def const(shape):
        return _resident(shape, lambda i: (0,) * len(shape))

    def tok(width):
        return pl.BlockSpec((tm, width), lambda i: (i, 0))

    dil_specs = [pl.BlockSpec((1, n_planes, tm, LANES),
                              lambda i: (i // tiles, 0, i % tiles, 0))] * N_DIL_GROUPS
    in_specs = [tok(D_MODEL)] + dil_specs + dil_specs + [tok(FOX_WIDTH), tok(2 * D_MODEL)]
    in_specs += [const((w, D_MODEL)), const((FOX_WIDTH, D_MODEL)),
                 const((D_MODEL, D_MODEL)), const((1, D_MODEL)),
                 const((D_MODEL, D_FF)), const((D_MODEL, D_FF)), const((D_FF, D_MODEL)),
                 const((1, D_MODEL))]
    return pl.pallas_call(
        _mix_ffn_kernel,
        grid=(n // tm,),
        in_specs=in_specs,
        out_specs=tok(D_MODEL),
        out_shape=jax.ShapeDtypeStruct((n, D_MODEL), F32),
        compiler_params=pltpu.CompilerParams(
            dimension_semantics=("arbitrary",), vmem_limit_bytes=VMEM_LIMIT),
        name="mix_ffn",
    )(x2d, *o_g, *lse_g, o_b, gates,
      w_dil_out.astype(BF16), w_fox_out.astype(BF16), w_out.astype(BF16),
      norm_ffn_g.reshape(1, -1),
      w_ffn_in[:, :D_FF].astype(BF16), w_ffn_in[:, D_FF:].astype(BF16),
      w_ffn_down.astype(BF16), norm_final_g.reshape(1, -1))


def kernel(x, norm_mix_g, w_in, b_fgt, b_gate, w_dil_out, w_fox_out, w_out, norm_ffn_g,
           w_ffn_in, w_ffn_down, norm_final_g):
    b, s, d = x.shape
    assert w_in.shape[0] == 1
    layer = 0
    a0, a1, a2, fq, fk, fv, c, c_t, gates = _inproj(
        x, norm_mix_g[layer], w_in[layer], b_fgt[layer], b_gate[layer], tm=512)
    o_g, lse_g = zip(*[_dilated(a, g) for g, a in enumerate((a0, a1, a2))])
    o_b = _fox(fq, fk, fv, c, c_t, tq=512)
    out = _mix_ffn(x.reshape(b * s, d), o_g, lse_g, o_b.reshape(b * s, FOX_WIDTH),
                   gates.reshape(b * s, 2 * D_MODEL), w_dil_out[layer], w_fox_out[layer],
                   w_out[layer], norm_ffn_g[layer], w_ffn_in[layer], w_ffn_down[layer],
                   norm_final_g, tm=512)
    return out.reshape(b, s, d)
```
